```python
import math
import jax
import jax.numpy as jnp
from jax import lax
import numpy as np

D_MODEL = 4096
BATCH = 4
SEQ = 2048
DEPTH = 1
DEC_BATCH = 32
DEC_SEQ = 4
PAST_LEN = 8192
PAGE_SIZE = 128

HEAD_DIM = 128
ATTN_HEADS = D_MODEL // (2 * HEAD_DIM)
KV_HEADS = ATTN_HEADS // 4
Q_PER_KV = ATTN_HEADS // KV_HEADS
ATTN_WIDTH = ATTN_HEADS * HEAD_DIM
KV_WIDTH = KV_HEADS * HEAD_DIM
ROT_DIM = HEAD_DIM // 4
ROPE_THETA = 500000.0
DILATED_BRANCHES = ((128, 1), (512, 4), (2048, 16))
MAX_REACH = max(w for w, _ in DILATED_BRANCHES)
Q_BLOCK = 128

SSD_WIDTH = D_MODEL // 2
SSD_HEAD_DIM = 64
SSD_HEADS = SSD_WIDTH // SSD_HEAD_DIM
SSD_GROUPS = 8
SSD_STATE = 128
CONV_WIDTH = 4
CONV_DIM = SSD_WIDTH + 2 * SSD_GROUPS * SSD_STATE
SSD_CHUNK = 128

MIX_WIDTH = ATTN_WIDTH + SSD_WIDTH
IN_SPLITS = (ATTN_WIDTH,
             ATTN_WIDTH + KV_WIDTH,
             ATTN_WIDTH + 2 * KV_WIDTH,
             ATTN_WIDTH + 2 * KV_WIDTH + SSD_WIDTH,
             ATTN_WIDTH + 2 * KV_WIDTH + SSD_WIDTH + CONV_DIM)
IN_WIDTH = IN_SPLITS[-1] + SSD_HEADS

N_EXPERTS = 32
TOP_K = 4
EXPERT_DIM = D_MODEL
SWIGLU_LIMIT = 7.0
SWIGLU_ALPHA = 1.702
MOE_BLOCK = 128

EPS = 1e-6

kernel_name = 'hybrid_dilated_attn_ssd_moe_adaln_step'


def _rmsnorm(x, w):
    xf = x.astype(jnp.float32)
    y = xf * lax.rsqrt(jnp.mean(xf * xf, axis=-1, keepdims=True) + EPS)
    return (y * w.astype(jnp.float32)).astype(x.dtype)


def _rope_partial(x, pos):
    half = ROT_DIM // 2
    inv_freq = jnp.power(ROPE_THETA, -jnp.arange(half, dtype=jnp.float32) * 2.0 / ROT_DIM)
    ang = pos.astype(jnp.float32)[:, None] * inv_freq[None, :]
    cos = jnp.cos(ang)[None, :, None, :]
    sin = jnp.sin(ang)[None, :, None, :]
    xf = x.astype(jnp.float32)
    x1 = xf[..., :half]
    x2 = xf[..., half:ROT_DIM]
    out = jnp.concatenate([x1 * cos - x2 * sin, x2 * cos + x1 * sin, xf[..., ROT_DIM:]], axis=-1)
    return out.astype(x.dtype)


def _dilated_attention(q, k_ctx, v_ctx, pos0):
    b, l = q.shape[:2]
    qb = math.gcd(l, Q_BLOCK)
    nq = l // qb
    q_blocks = jnp.moveaxis(q.reshape(b, nq, qb, KV_HEADS, Q_PER_KV, HEAD_DIM), 1, 0)
    scale = HEAD_DIM ** -0.5

    def one_block(args):
        q_blk, blk = args
        qi = blk * qb + jnp.arange(qb, dtype=jnp.int32)
        qf = q_blk.astype(jnp.float32) * scale
        lses = []
        outs = []
        for window, dil in DILATED_BRANCHES:
            offs = jnp.arange(0, window + 1, dil, dtype=jnp.int32)
            rows = MAX_REACH + qi[:, None] - offs[None, :]
            valid = (pos0 + qi[:, None] - offs[None, :]) >= 0
            kg = k_ctx[:, rows].astype(jnp.float32)
            vg = v_ctx[:, rows].astype(jnp.float32)
            s = jnp.einsum('bqhgd,bqjhd->bqhgj', qf, kg)
            s = jnp.where(valid[None, :, None, None, :], s, -jnp.inf)
            m = jnp.max(s, axis=-1, keepdims=True)
            e = jnp.exp(s - m)
            den = jnp.sum(e, axis=-1)
            outs.append(jnp.einsum('bqhgj,bqjhd->bqhgd', e, vg) / den[..., None])
            lses.append(m[..., 0] + jnp.log(den))
        wts = jax.nn.softmax(jnp.stack(lses), axis=0)
        return jnp.sum(wts[..., None] * jnp.stack(outs), axis=0)

    out = lax.map(one_block, (q_blocks, jnp.arange(nq, dtype=jnp.int32)))
    out = jnp.moveaxis(out, 0, 1).reshape(b, l, ATTN_WIDTH)
    return out.astype(q.dtype)


def _ssd_scan(x, dt, a, bm, cm, h0):
    b, l = x.shape[:2]
    lc = math.gcd(l, SSD_CHUNK)
    nc = l // lc
    r = SSD_HEADS // SSD_GROUPS
    xc = x.reshape(b, nc, lc, SSD_GROUPS, r, SSD_HEAD_DIM)
    dtc = dt.reshape(b, nc, lc, SSD_GROUPS, r)
    bc = bm.reshape(b, nc, lc, SSD_GROUPS, SSD_STATE)
    cc = cm.reshape(b, nc, lc, SSD_GROUPS, SSD_STATE)
    acs = jnp.moveaxis(jnp.cumsum(dtc * a.reshape(SSD_GROUPS, r), axis=2), 2, -1)
    xdt = xc * dtc[..., None]
    causal = jnp.tril(jnp.ones((lc, lc), dtype=bool))
    decay_in = jnp.exp(jnp.where(causal, acs[..., :, None] - acs[..., None, :], -jnp.inf))
    cb = jnp.einsum('bclgn,bcsgn->bcgls', cc, bc)
    y_diag = jnp.einsum('bcgrls,bcsgrp->bclgrp', cb[:, :, :, None] * decay_in, xdt)
    decay_to_end = jnp.moveaxis(jnp.exp(acs[..., -1:] - acs), -1, 2)
    chunk_states = jnp.einsum('bcsgn,bcsgrp->bcgrpn', bc, xdt * decay_to_end[..., None])
    chunk_decay = jnp.exp(acs[..., -1])

    def step(h, inp):
        st, dec = inp
        return h * dec[..., None, None] + st, h

    h_init = h0.reshape(b, SSD_GROUPS, r, SSD_HEAD_DIM, SSD_STATE)
    h_final, h_prev = lax.scan(step, h_init, (jnp.moveaxis(chunk_states, 1, 0), jnp.moveaxis(chunk_decay, 1, 0)))
    h_prev = jnp.moveaxis(h_prev, 0, 1)
    decay_from_start = jnp.moveaxis(jnp.exp(acs), -1, 2)
    y_off = jnp.einsum('bclgn,bcgrpn->bclgrp', cc, h_prev) * decay_from_start[..., None]
    y = (y_diag + y_off).reshape(b, l, SSD_HEADS, SSD_HEAD_DIM)
    return y, h_final.reshape(b, SSD_HEADS, SSD_HEAD_DIM, SSD_STATE)


def _ssd_mixer(z, xbc, dt_raw, conv_state, ssm_state, conv_w, conv_b, dt_bias, a_log, d_skip, ssd_norm_w):
    b, l = xbc.shape[:2]
    full = jnp.concatenate([conv_state.astype(xbc.dtype), xbc], axis=1)
    conv = conv_b + sum(full[:, j:j + l] * conv_w[j] for j in range(CONV_WIDTH))
    new_conv = full[:, l:]
    u = jax.nn.silu(conv.astype(jnp.float32))
    xs, bm, cm = jnp.split(u, [SSD_WIDTH, SSD_WIDTH + SSD_GROUPS * SSD_STATE], axis=-1)
    dt = jax.nn.softplus(dt_raw.astype(jnp.float32) + dt_bias.astype(jnp.float32))
    a = -jnp.exp(a_log.astype(jnp.float32))
    xh = xs.reshape(b, l, SSD_HEADS, SSD_HEAD_DIM)
    y, h_final = _ssd_scan(xh, dt, a,
                           bm.reshape(b, l, SSD_GROUPS, SSD_STATE),
                           cm.reshape(b, l, SSD_GROUPS, SSD_STATE),
                           ssm_state.astype(jnp.float32))
    y = y + xh * d_skip.astype(jnp.float32)[:, None]
    y = y.reshape(b, l, SSD_WIDTH) * jax.nn.silu(z.astype(jnp.float32))
    yg = y.reshape(b, l, SSD_GROUPS, SSD_WIDTH // SSD_GROUPS)
    yg = yg * lax.rsqrt(jnp.mean(yg * yg, axis=-1, keepdims=True) + EPS)
    y = yg.reshape(b, l, SSD_WIDTH) * ssd_norm_w.astype(jnp.float32)
    return y.astype(z.dtype), new_conv, h_final.astype(ssm_state.dtype)


def _moe(h, router_w, router_b, w_gate_up, b_gate_up, w_down, b_down):
    n, d = h.shape
    m = n * TOP_K
    logits = jnp.dot(h, router_w).astype(jnp.float32) + router_b.astype(jnp.float32)
    top_val, top_idx = lax.top_k(logits, TOP_K)
    gates = jax.nn.softmax(top_val, axis=-1)
    flat_e = top_idx.reshape(m)
    flat_tok = jnp.broadcast_to(jnp.arange(n, dtype=jnp.int32)[:, None], (n, TOP_K)).reshape(m)
    order = jnp.argsort(flat_e)
    sort_e = flat_e[order]
    sort_tok = flat_tok[order]
    sort_g = gates.reshape(m)[order]
    counts = jnp.bincount(flat_e, length=N_EXPERTS)
    start = jnp.cumsum(counts) - counts
    padded = (counts + MOE_BLOCK - 1) // MOE_BLOCK * MOE_BLOCK
    pend = jnp.cumsum(padded)
    pstart = pend - padded
    dest = pstart[sort_e] + jnp.arange(m, dtype=jnp.int32) - start[sort_e]
    n_blocks = -(-m // MOE_BLOCK) + N_EXPERTS
    n_slots = n_blocks * MOE_BLOCK
    slot_tok = jnp.full((n_slots,), n, dtype=jnp.int32).at[dest].set(sort_tok)
    block_e = jnp.minimum(jnp.searchsorted(pend, jnp.arange(n_blocks, dtype=jnp.int32) * MOE_BLOCK, side='right'),
                          N_EXPERTS - 1)
    h_ext = jnp.concatenate([h, jnp.zeros((1, d), h.dtype)], axis=0)
    x_blocks = h_ext[slot_tok].reshape(n_blocks, MOE_BLOCK, d)

    def expert_block(args):
        xb, e = args
        gu = xb @ w_gate_up[e] + b_gate_up[e]
        g, u = jnp.split(gu, 2, axis=-1)
        g = jnp.minimum(g, SWIGLU_LIMIT)
        u = jnp.clip(u, -SWIGLU_LIMIT, SWIGLU_LIMIT)
        act = g * jax.nn.sigmoid(SWIGLU_ALPHA * g) * (u + 1)
        return act @ w_down[e] + b_down[e]

    y_slots = lax.map(expert_block, (x_blocks, block_e)).reshape(n_slots, d)
    y_pairs = y_slots[dest] * sort_g[:, None].astype(y_slots.dtype)
    return jax.ops.segment_sum(y_pairs, sort_tok, num_segments=n)


def _layer(x, c, pos0, k_past, v_past, conv_state, ssm_state, p):
    (w_ada, b_ada, norm_mix_w, norm_ffn_w, w_in, q_norm_w, k_norm_w, conv_w, conv_b, dt_bias, a_log,
     d_skip, ssd_norm_w, w_out, router_w, router_b, w_gate_up, b_gate_up, w_down, b_down) = p
    b, l, _ = x.shape
    mod = jax.nn.silu(c) @ w_ada + b_ada
    sh1, sc1, g1, sh2, sc2, g2 = jnp.split(mod[:, None, :], 6, axis=-1)
    h = _rmsnorm(x, norm_mix_w) * (1 + sc1) + sh1
    proj = h @ w_in
    q, k, v, z, xbc, dt_raw = jnp.split(proj, IN_SPLITS, axis=-1)
    q = _rmsnorm(q.reshape(b, l, ATTN_HEADS, HEAD_DIM), q_norm_w)
    k = _rmsnorm(k.reshape(b, l, KV_HEADS, HEAD_DIM), k_norm_w)
    v = v.reshape(b, l, KV_HEADS, HEAD_DIM)
    pos = pos0 + jnp.arange(l, dtype=jnp.int32)
    q = _rope_partial(q, pos)
    k = _rope_partial(k, pos)
    pad = jnp.zeros((b, MAX_REACH - k_past.shape[1], KV_HEADS, HEAD_DIM), k.dtype)
    k_ctx = jnp.concatenate([pad, k_past.astype(k.dtype), k], axis=1)
    v_ctx = jnp.concatenate([pad, v_past.astype(v.dtype), v], axis=1)
    attn = _dilated_attention(q, k_ctx, v_ctx, pos0)
    ssd, new_conv, new_ssm = _ssd_mixer(z, xbc, dt_raw, conv_state, ssm_state, conv_w, conv_b,
                                        dt_bias, a_log, d_skip, ssd_norm_w)
    x = x + g1 * (jnp.concatenate([attn, ssd], axis=-1) @ w_out)
    h2 = _rmsnorm(x, norm_ffn_w) * (1 + sc2) + sh2
    ff = _moe(h2.reshape(b * l, D_MODEL), router_w, router_b, w_gate_up, b_gate_up, w_down, b_down)
    x = x + g2 * ff.reshape(b, l, D_MODEL)
    return x, k, v, new_conv, new_ssm


def setup_inputs(seed: int = 0) -> dict:
    key = jax.random.key(seed)
    ks = jax.random.split(key, 28)
    f32 = jnp.float32

    def nrm(k, shape, scale):
        return jax.random.normal(k, shape, f32) * scale

    kv_win = min(MAX_REACH, PAST_LEN)
    dt0 = jnp.exp(jax.random.uniform(ks[17], (DEPTH, SSD_HEADS), f32, math.log(1e-3), math.log(1e-1)))
    return {
        'x_prompt': nrm(ks[0], (BATCH, SEQ, D_MODEL), 1.0),
        'x_sample': nrm(ks[1], (DEC_BATCH, DEC_SEQ, D_MODEL), 1.0),
        'cache_k': nrm(ks[2], (DEPTH, DEC_BATCH, kv_win, KV_HEADS, HEAD_DIM), 1.0),
        'cache_v': nrm(ks[3], (DEPTH, DEC_BATCH, kv_win, KV_HEADS, HEAD_DIM), 1.0),
        'state_conv': nrm(ks[4], (DEPTH, DEC_BATCH, CONV_WIDTH - 1, CONV_DIM), 1.0),
        'state_ssm': nrm(ks[5], (DEPTH, DEC_BATCH, SSD_HEADS, SSD_HEAD_DIM, SSD_STATE), 0.1),
        'c_prompt': nrm(ks[6], (BATCH, D_MODEL), 1.0),
        'c_sample': nrm(ks[7], (DEC_BATCH, D_MODEL), 1.0),
        'w_ada': nrm(ks[8], (DEPTH, D_MODEL, 6 * D_MODEL), 0.5 * D_MODEL ** -0.5),
        'b_ada': nrm(ks[9], (DEPTH, 6 * D_MODEL), 0.01),
        'norm_mix_w': 1.0 + nrm(ks[10], (DEPTH, D_MODEL), 0.02),
        'norm_ffn_w': 1.0 + nrm(ks[11], (DEPTH, D_MODEL), 0.02),
        'w_in': nrm(ks[12], (DEPTH, D_MODEL, IN_WIDTH), D_MODEL ** -0.5),
        'q_norm_w': 1.0 + nrm(ks[13], (DEPTH, HEAD_DIM), 0.02),
        'k_norm_w': 1.0 + nrm(ks[14], (DEPTH, HEAD_DIM), 0.02),
        'conv_w': nrm(ks[15], (DEPTH, CONV_WIDTH, CONV_DIM), CONV_WIDTH ** -0.5),
        'conv_b': nrm(ks[16], (DEPTH, CONV_DIM), 0.01),
        'dt_bias': dt0 + jnp.log(-jnp.expm1(-dt0)),
        'a_log': jnp.log(jax.random.uniform(ks[18], (DEPTH, SSD_HEADS), f32, 1.0, 16.0)),
        'd_skip': 1.0 + nrm(ks[19], (DEPTH, SSD_HEADS), 0.1),
        'ssd_norm_w': 1.0 + nrm(ks[20], (DEPTH, SSD_WIDTH), 0.02),
        'w_out': nrm(ks[21], (DEPTH, MIX_WIDTH, D_MODEL), MIX_WIDTH ** -0.5),
        'router_w': nrm(ks[22], (DEPTH, D_MODEL, N_EXPERTS), D_MODEL ** -0.5),
        'router_b': nrm(ks[23], (DEPTH, N_EXPERTS), 0.01),
        'w_gate_up': nrm(ks[24], (DEPTH, N_EXPERTS, D_MODEL, 2 * EXPERT_DIM), D_MODEL ** -0.5),
        'b_gate_up': nrm(ks[25], (DEPTH, N_EXPERTS, 2 * EXPERT_DIM), 0.01),
        'w_down': nrm(ks[26], (DEPTH, N_EXPERTS, EXPERT_DIM, D_MODEL), EXPERT_DIM ** -0.5),
        'b_down': nrm(ks[27], (DEPTH, N_EXPERTS, D_MODEL), 0.01),
    }


def reference(x_prompt, x_sample, cache_k, cache_v, state_conv, state_ssm, c_prompt, c_sample,
              w_ada, b_ada, norm_mix_w, norm_ffn_w, w_in, q_norm_w, k_norm_w, conv_w, conv_b,
              dt_bias, a_log, d_skip, ssd_norm_w, w_out, router_w, router_b,
              w_gate_up, b_gate_up, w_down, b_down):
    b_p, seq = x_prompt.shape[:2]
    keep_p = min(MAX_REACH, seq)
    y_prompt = x_prompt
    y_sample = x_sample
    k_p, v_p, cv_p, ss_p = [], [], [], []
    k_s, v_s, cv_s, ss_s = [], [], [], []
    for layer in range(DEPTH):
        params = (w_ada[layer], b_ada[layer], norm_mix_w[layer], norm_ffn_w[layer], w_in[layer],
                  q_norm_w[layer], k_norm_w[layer], conv_w[layer], conv_b[layer], dt_bias[layer],
                  a_log[layer], d_skip[layer], ssd_norm_w[layer], w_out[layer], router_w[layer],
                  router_b[layer], w_gate_up[layer], b_gate_up[layer], w_down[layer], b_down[layer])
        empty_kv = jnp.zeros((b_p, 0, KV_HEADS, HEAD_DIM), x_prompt.dtype)
        zero_conv = jnp.zeros((b_p, CONV_WIDTH - 1, CONV_DIM), state_conv.dtype)
        zero_ssm = jnp.zeros((b_p, SSD_HEADS, SSD_HEAD_DIM, SSD_STATE), state_ssm.dtype)
        y_prompt, kp, vp, cp, sp = _layer(y_prompt, c_prompt, 0, empty_kv, empty_kv, zero_conv, zero_ssm, params)
        y_sample, kn, vn, cn, sn = _layer(y_sample, c_sample, PAST_LEN, cache_k[layer], cache_v[layer],
                                          state_conv[layer], state_ssm[layer], params)
        k_p.append(kp[:, seq - keep_p:])
        v_p.append(vp[:, seq - keep_p:])
        cv_p.append(cp)
        ss_p.append(sp)
        k_s.append(kn)
        v_s.append(vn)
        cv_s.append(cn)
        ss_s.append(sn)
    return (y_prompt, y_sample,
            jnp.stack(k_p), jnp.stack(v_p), jnp.stack(cv_p), jnp.stack(ss_p),
            jnp.stack(k_s), jnp.stack(v_s), jnp.stack(cv_s), jnp.stack(ss_s))
```

```python
import functools
import math

import jax
import jax.numpy as jnp
from jax import lax
from jax.experimental import pallas as pl
from jax.experimental.pallas import tpu as pltpu

F32 = jnp.float32
BF = jnp.bfloat16
I32 = jnp.int32

D_MODEL = 4096
PAST_LEN = 8192
HEAD_DIM = 128
ATTN_HEADS = 16
KV_HEADS = 4
Q_PER_KV = ATTN_HEADS // KV_HEADS
ATTN_WIDTH = ATTN_HEADS * HEAD_DIM
KV_WIDTH = KV_HEADS * HEAD_DIM
ROT_DIM = HEAD_DIM // 4
ROPE_THETA = 500000.0
DILATED_BRANCHES = ((128, 1), (512, 4), (2048, 16))
MAX_REACH = 2048
SSD_WIDTH = 2048
SSD_HEAD_DIM = 64
SSD_HEADS = 32
SSD_GROUPS = 8
SSD_STATE = 128
CONV_WIDTH = 4
CONV_DIM = SSD_WIDTH + 2 * SSD_GROUPS * SSD_STATE
SSD_CHUNK = 128
MIX_WIDTH = ATTN_WIDTH + SSD_WIDTH
N_EXPERTS = 32
TOP_K = 4
EXPERT_DIM = D_MODEL
SWIGLU_LIMIT = 7.0
SWIGLU_ALPHA = 1.702
EPS = 1e-6

LANES = 128
SUBLANES = 8
NEG = -1e30
QKV_WIDTH = ATTN_WIDTH + 2 * KV_WIDTH
DT_COL0 = QKV_WIDTH + SSD_WIDTH + CONV_DIM
MOE_TILE = 256
MOE_UP_TN = 512
MOE_DOWN_TN = 1024
VMEM_LIMIT = 56 * 2 ** 20
HI = lax.Precision.HIGHEST
NT_DIMS = (((1,), (1,)), ((), ()))
TN_DIMS = (((0,), (0,)), ((), ()))


def _params(*sem):
    return pltpu.CompilerParams(dimension_semantics=sem, vmem_limit_bytes=VMEM_LIMIT)


def _silu(x):
    return x * jax.nn.sigmoid(x)


def _ada_kernel(c_ref, w_ref, b_ref, o_ref):
    a = _silu(c_ref[...]).astype(BF)
    o_ref[...] = jnp.dot(a, w_ref[...].astype(BF), preferred_element_type=F32) + b_ref[...]


def _ada(c, w_ada, b_ada3, layer):
    mc = c.shape[0]
    n = 6 * D_MODEL
    tn = 512
    return pl.pallas_call(
        _ada_kernel,
        grid=(n // tn,),
        in_specs=[pl.BlockSpec((mc, D_MODEL), lambda j: (0, 0)),
                  pl.BlockSpec((None, D_MODEL, tn), lambda j: (layer, 0, j)),
                  pl.BlockSpec((None, 1, tn), lambda j: (layer, 0, j))],
        out_specs=pl.BlockSpec((mc, tn), lambda j: (0, j)),
        out_shape=jax.ShapeDtypeStruct((mc, n), F32),
        compiler_params=_params("arbitrary"),
        name="ada",
    )(c, w_ada, b_ada3)


def _norm_mod_kernel(x_ref, w_ref, sc_ref, sh_ref, o_ref):
    x = x_ref[...]
    y = x * lax.rsqrt(jnp.mean(x * x, axis=-1, keepdims=True) + EPS) * w_ref[...]
    o_ref[...] = (y * (1.0 + sc_ref[...]) + sh_ref[...]).astype(o_ref.dtype)


def _norm_mod(x3, norm_w3, layer, mod, k_sh, k_sc, tm):
    b, l, _ = x3.shape
    per_tok = mod.shape[1] > 1
    mblk = (None, tm if per_tok else 1, D_MODEL)

    def mspec(k):
        return pl.BlockSpec(mblk, lambda bi, i: (bi, i if per_tok else 0, k))

    return pl.pallas_call(
        _norm_mod_kernel,
        grid=(b, l // tm),
        in_specs=[pl.BlockSpec((None, tm, D_MODEL), lambda bi, i: (bi, i, 0)),
                  pl.BlockSpec((None, 1, D_MODEL), lambda bi, i: (layer, 0, 0)),
                  mspec(k_sc), mspec(k_sh)],
        out_specs=pl.BlockSpec((None, tm, D_MODEL), lambda bi, i: (bi, i, 0)),
        out_shape=jax.ShapeDtypeStruct((b, l, D_MODEL), BF),
        compiler_params=_params("arbitrary", "arbitrary"),
        name="norm_mod",
    )(x3, norm_w3, mod, mod)


def _mm_kernel(a_ref, w_ref, o_ref, wb_ref):
    @pl.when(pl.program_id(1) == 0)
    def _():
        wb_ref[...] = w_ref[...].astype(BF)

    o_ref[...] = jnp.dot(a_ref[...], wb_ref[...], preferred_element_type=F32)


def _matmul(a, w3, layer, col_blk0, n_tiles, tn, tm):
    m, k = a.shape
    return pl.pallas_call(
        _mm_kernel,
        grid=(n_tiles, m // tm),
        in_specs=[pl.BlockSpec((tm, k), lambda j, i: (i, 0)),
                  pl.BlockSpec((None, k, tn), lambda j, i: (layer, 0, col_blk0 + j))],
        out_specs=pl.BlockSpec((tm, tn), lambda j, i: (i, j)),
        out_shape=jax.ShapeDtypeStruct((m, n_tiles * tn), F32),
        scratch_shapes=[pltpu.VMEM((k, tn), BF)],
        compiler_params=_params("arbitrary", "arbitrary"),
        name="matmul",
    )(a, w3)


def _rope_tables(pos):
    half = ROT_DIM // 2
    inv_freq = jnp.power(ROPE_THETA, -jnp.arange(half, dtype=F32) * 2.0 / ROT_DIM)
    ang = pos.astype(F32)[:, None] * inv_freq[None, :]
    cos, sin = jnp.cos(ang), jnp.sin(ang)
    n = pos.shape[0]
    cosf = jnp.concatenate([cos, cos, jnp.ones((n, HEAD_DIM - ROT_DIM), F32)], axis=1)
    sinf = jnp.concatenate([-sin, sin, jnp.zeros((n, HEAD_DIM - ROT_DIM), F32)], axis=1)
    return cosf, sinf


def _norm_rope_head(xh, w, cosf, sinf, lane):
    half = ROT_DIM // 2
    y = xh * lax.rsqrt(jnp.mean(xh * xh, axis=-1, keepdims=True) + EPS) * w
    partner = jnp.where(lane < half, pltpu.roll(y, HEAD_DIM - half, 1), pltpu.roll(y, half, 1))
    return y * cosf + partner * sinf


def _qk_kernel(q_ref, k_ref, cos_ref, sin_ref, qw_ref, kw_ref, qo_ref, ko_ref):
    cosf = cos_ref[...]
    sinf = sin_ref[...]
    lane = lax.broadcasted_iota(I32, cosf.shape, 1)
    scale = HEAD_DIM ** -0.5
    for h in range(ATTN_HEADS):
        sl = slice(h * HEAD_DIM, (h + 1) * HEAD_DIM)
        qo_ref[:, sl] = _norm_rope_head(q_ref[:, sl], qw_ref[...], cosf, sinf, lane) * scale
    for h in range(KV_HEADS):
        sl = slice(h * HEAD_DIM, (h + 1) * HEAD_DIM)
        ko_ref[:, sl] = _norm_rope_head(k_ref[:, sl], kw_ref[...], cosf, sinf, lane)


def _qk_norm_rope(qkv, cosf, sinf, q_norm_w3, k_norm_w3, layer, tm):
    m = qkv.shape[0]
    return pl.pallas_call(
        _qk_kernel,
        grid=(m // tm,),
        in_specs=[pl.BlockSpec((tm, ATTN_WIDTH), lambda i: (i, 0)),
                  pl.BlockSpec((tm, KV_WIDTH), lambda i: (i, ATTN_WIDTH // KV_WIDTH)),
                  pl.BlockSpec((tm, HEAD_DIM), lambda i: (i, 0)),
                  pl.BlockSpec((tm, HEAD_DIM), lambda i: (i, 0)),
                  pl.BlockSpec((None, 1, HEAD_DIM), lambda i: (layer, 0, 0)),
                  pl.BlockSpec((None, 1, HEAD_DIM), lambda i: (layer, 0, 0))],
        out_specs=[pl.BlockSpec((tm, ATTN_WIDTH), lambda i: (i, 0)),
                   pl.BlockSpec((tm, KV_WIDTH), lambda i: (i, 0))],
        out_shape=[jax.ShapeDtypeStruct((m, ATTN_WIDTH), F32),
                   jax.ShapeDtypeStruct((m, KV_WIDTH), F32)],
        compiler_params=_params("arbitrary"),
        name="qk_norm_rope",
    )(qkv, qkv, cosf, sinf, q_norm_w3, k_norm_w3)


def _multiplicity(delta):
    ge0 = delta >= 0
    cnt = jnp.zeros(delta.shape, F32)
    for window, dil in DILATED_BRANCHES:
        hit = ge0 & (delta <= window) & ((delta & (dil - 1)) == 0)
        cnt = cnt + jnp.where(hit, 1.0, 0.0)
    return cnt


def _attn_prompt_kernel(q_ref, k_ref, v_ref, o_ref):
    c = pl.program_id(2)
    qb = q_ref.shape[0]
    rows = Q_PER_KV * qb
    q = q_ref[...]
    qs = jnp.concatenate([q[:, h * HEAD_DIM:(h + 1) * HEAD_DIM] for h in range(Q_PER_KV)], axis=0).astype(BF)
    ri = lax.broadcasted_iota(I32, (rows, qb), 0) & (qb - 1)
    cj = lax.broadcasted_iota(I32, (rows, qb), 1)
    d0 = ri - cj
    n_back = jnp.minimum(c, MAX_REACH // qb)

    def body(t, carry):
        m, l, acc = carry
        start = pl.multiple_of((c - t) * qb, qb)
        kblk = k_ref[pl.ds(start, qb), :].astype(BF)
        vblk = v_ref[pl.ds(start, qb), :].astype(BF)
        s = lax.dot_general(qs, kblk, NT_DIMS, preferred_element_type=F32)
        cnt = _multiplicity(d0 + t * qb)
        sm = jnp.where(cnt > 0.0, s, NEG)
        m_new = jnp.maximum(m, jnp.max(sm, axis=-1, keepdims=True))
        p = cnt * jnp.exp(sm - m_new)
        alpha = jnp.exp(m - m_new)
        l = alpha * l + jnp.sum(p, axis=-1, keepdims=True)
        acc = alpha * acc + jnp.dot(p.astype(BF), vblk, preferred_element_type=F32)
        return m_new, l, acc

    init = (jnp.full((rows, 1), NEG, F32), jnp.zeros((rows, 1), F32), jnp.zeros((rows, HEAD_DIM), F32))
    _, l, acc = lax.fori_loop(0, n_back + 1, body, init)
    out = acc / l
    o_ref[...] = jnp.concatenate([out[h * qb:(h + 1) * qb] for h in range(Q_PER_KV)], axis=1).astype(o_ref.dtype)


def _attn_prompt(qn3, kn3, qkv3):
    b, l, _ = qn3.shape
    qb = math.gcd(l, 128)
    gw = Q_PER_KV * HEAD_DIM
    v_blk0 = (ATTN_WIDTH + KV_WIDTH) // HEAD_DIM
    return pl.pallas_call(
        _attn_prompt_kernel,
        grid=(b, KV_HEADS, l // qb),
        in_specs=[pl.BlockSpec((None, qb, gw), lambda bi, g, c: (bi, c, g)),
                  pl.BlockSpec((None, l, HEAD_DIM), lambda bi, g, c: (bi, 0, g)),
                  pl.BlockSpec((None, l, HEAD_DIM), lambda bi, g, c: (bi, 0, v_blk0 + g))],
        out_specs=pl.BlockSpec((None, qb, gw), lambda bi, g, c: (bi, c, g)),
        out_shape=jax.ShapeDtypeStruct((b, l, ATTN_WIDTH), BF),
        compiler_params=_params("arbitrary", "arbitrary", "arbitrary"),
        name="attn_prompt",
    )(qn3, kn3, qkv3)


def _attn_sample_kernel(q_ref, ck_ref, cv_ref, kn_ref, vn_ref, o_ref, *, n_new):
    w = ck_ref.shape[0]
    pad = kn_ref.shape[0]
    rows = q_ref.shape[1]
    ri = lax.broadcasted_iota(I32, (rows, w + pad), 0) & (n_new - 1)
    cj = lax.broadcasted_iota(I32, (rows, w + pad), 1)
    cnt = _multiplicity(w + ri - cj)
    for g in range(KV_HEADS):
        sl = slice(g * HEAD_DIM, (g + 1) * HEAD_DIM)
        qg = q_ref[g].astype(BF)
        s_c = lax.dot_general(qg, ck_ref[:, sl].astype(BF), NT_DIMS, preferred_element_type=F32)
        s_n = lax.dot_general(qg, kn_ref[:, sl].astype(BF), NT_DIMS, preferred_element_type=F32)
        s = jnp.concatenate([s_c, s_n], axis=1)
        sm = jnp.where(cnt > 0.0, s, NEG)
        m = jnp.max(sm, axis=-1, keepdims=True)
        p = cnt * jnp.exp(sm - m)
        l = jnp.sum(p, axis=-1, keepdims=True)
        pb = p.astype(BF)
        o = (jnp.dot(pb[:, :w], cv_ref[:, sl].astype(BF), preferred_element_type=F32)
             + jnp.dot(pb[:, w:], vn_ref[:, sl].astype(BF), preferred_element_type=F32))
        o_ref[g] = o / l


def _attn_sample(qg4, ck3, cv3, knp3, vnp3, n_new):
    b, _, rows, _ = qg4.shape
    w = ck3.shape[1]
    pad = knp3.shape[1]
    return pl.pallas_call(
        functools.partial(_attn_sample_kernel, n_new=n_new),
        grid=(b,),
        in_specs=[pl.BlockSpec((None, KV_HEADS, rows, HEAD_DIM), lambda i: (i, 0, 0, 0)),
                  pl.BlockSpec((None, w, KV_WIDTH), lambda i: (i, 0, 0)),
                  pl.BlockSpec((None, w, KV_WIDTH), lambda i: (i, 0, 0)),
                  pl.BlockSpec((None, pad, KV_WIDTH), lambda i: (i, 0, 0)),
                  pl.BlockSpec((None, pad, KV_WIDTH), lambda i: (i, 0, 0))],
        out_specs=pl.BlockSpec((None, KV_HEADS, rows, HEAD_DIM), lambda i: (i, 0, 0, 0)),
        out_shape=jax.ShapeDtypeStruct((b, KV_HEADS, rows, HEAD_DIM), F32),
        compiler_params=_params("arbitrary"),
        name="attn_sample",
    )(qg4, ck3, cv3, knp3, vnp3)


def _ssd_kernel(x_ref, z_ref, dtr_ref, cw_ref, cb_ref, dtb_ref, alog_ref, dsk_ref, nw_ref, cst_ref, sst_ref,
                e_ref, y_ref, hout_ref, h_ref, tail_ref, buf_ref, *, valid):
    lc = x_ref.shape[0]
    c = pl.program_id(1)
    gw = SSD_WIDTH // SSD_GROUPS
    hpg = SSD_HEADS // SSD_GROUPS
    b_col0 = SSD_WIDTH
    c_col0 = SSD_WIDTH + SSD_GROUPS * SSD_STATE

    @pl.when(c == 0)
    def _():
        h_ref[...] = sst_ref[...]
        tail_ref[...] = cst_ref[...]

    x = x_ref[...]
    buf_ref[0:SUBLANES, :] = tail_ref[...]
    buf_ref[SUBLANES:SUBLANES + lc, :] = x
    tail_ref[...] = x[lc - SUBLANES:lc, :]
    conv = cb_ref[...]
    for j in range(CONV_WIDTH):
        r0 = SUBLANES - (CONV_WIDTH - 1) + j
        conv = conv + cw_ref[j:j + 1, :] * buf_ref[r0:r0 + lc, :]
    u = _silu(conv)
    xs = u[:, :SSD_WIDTH]

    dtv = dtr_ref[...] + dtb_ref[...]
    dt = jnp.maximum(dtv, 0.0) + jnp.log1p(jnp.exp(-jnp.abs(dtv)))
    row = lax.broadcasted_iota(I32, (lc, lc), 0)
    col = lax.broadcasted_iota(I32, (lc, lc), 1)
    if valid < lc:
        dt = jnp.where(lax.broadcasted_iota(I32, dt.shape, 0) < valid, dt, 0.0)
    da = dt * (-jnp.exp(alog_ref[...]))
    causal = row >= col
    acs = jnp.dot(jnp.where(causal, 1.0, 0.0), da, precision=HI, preferred_element_type=F32)
    acs_t = acs.T
    last = acs[lc - 1:lc, :]
    expand = e_ref[...]
    dt_e = jnp.dot(dt, expand, precision=HI, preferred_element_type=F32)
    dfs_e = jnp.dot(jnp.exp(acs), expand, precision=HI, preferred_element_type=F32)
    dte_e = jnp.dot(jnp.exp(last - acs), expand, precision=HI, preferred_element_type=F32)
    xdt = xs * dt_e
    xdtw = xdt * dte_e
    lane = lax.broadcasted_iota(I32, (lc, LANES), 1)
    z = z_ref[...]

    for g in range(SSD_GROUPS):
        gs = slice(g * gw, (g + 1) * gw)
        bg = u[:, b_col0 + g * SSD_STATE:b_col0 + (g + 1) * SSD_STATE].astype(BF)
        cg = u[:, c_col0 + g * SSD_STATE:c_col0 + (g + 1) * SSD_STATE].astype(BF)
        cb = lax.dot_general(cg, bg, NT_DIMS, preferred_element_type=F32)
        hg = h_ref[g * gw:(g + 1) * gw, :]
        y_off = lax.dot_general(cg, hg.astype(BF), NT_DIMS, preferred_element_type=F32)
        pieces = []
        for pr in range(hpg // 2):
            xp = xdt[:, g * gw + pr * LANES:g * gw + (pr + 1) * LANES]
            acc = jnp.zeros((lc, LANES), F32)
            for hh in range(2):
                h = g * hpg + pr * 2 + hh
                diff = acs[:, h:h + 1] - acs_t[h:h + 1, :]
                decay = jnp.exp(jnp.where(causal, diff, -jnp.inf))
                mat = (cb * decay).astype(BF)
                keep = (lane < SSD_HEAD_DIM) if hh == 0 else (lane >= SSD_HEAD_DIM)
                acc = acc + jnp.dot(mat, jnp.where(keep, xp, 0.0).astype(BF), preferred_element_type=F32)
            pieces.append(acc)
        y_diag = jnp.concatenate(pieces, axis=1)
        yg = y_diag + y_off * dfs_e[:, gs] + xs[:, gs] * dsk_ref[:, gs]

        ds = lax.dot_general(xdtw[:, gs].astype(BF), bg, TN_DIMS, preferred_element_type=F32)
        for hh in range(hpg):
            h = g * hpg + hh
            hs = slice(hh * SSD_HEAD_DIM, (hh + 1) * SSD_HEAD_DIM)
            h_ref[g * gw + hh * SSD_HEAD_DIM:g * gw + (hh + 1) * SSD_HEAD_DIM, :] = (
                hg[hs] * jnp.exp(last[:, h:h + 1]) + ds[hs])

        yg = yg * _silu(z[:, gs])
        yg = yg * lax.rsqrt(jnp.mean(yg * yg, axis=-1, keepdims=True) + EPS)
        y_ref[:, gs] = (yg * nw_ref[:, gs]).astype(y_ref.dtype)

    @pl.when(c == pl.num_programs(1) - 1)
    def _():
        hout_ref[...] = h_ref[...]


def _ssd(xbc3, z3, dtr3, conv_st8, ssm_st3, ssd_consts, layer, valid):
    conv_w, conv_b3, dtb3, alog3, dsk3, nw3, expand = ssd_consts
    b, l, _ = xbc3.shape
    lc = SSD_CHUNK
    vec = lambda wdt: pl.BlockSpec((None, 1, wdt), lambda bi, c: (layer, 0, 0))
    return pl.pallas_call(
        functools.partial(_ssd_kernel, valid=valid),
        grid=(b, l // lc),
        in_specs=[pl.BlockSpec((None, lc, CONV_DIM), lambda bi, c: (bi, c, 0)),
                  pl.BlockSpec((None, lc, SSD_WIDTH), lambda bi, c: (bi, c, 0)),
                  pl.BlockSpec((None, lc, LANES), lambda bi, c: (bi, c, 0)),
                  pl.BlockSpec((None, CONV_WIDTH, CONV_DIM), lambda bi, c: (layer, 0, 0)),
                  vec(CONV_DIM), vec(LANES), vec(LANES), vec(SSD_WIDTH), vec(SSD_WIDTH),
                  pl.BlockSpec((None, SUBLANES, CONV_DIM), lambda bi, c: (bi, 0, 0)),
                  pl.BlockSpec((None, SSD_HEADS * SSD_HEAD_DIM, SSD_STATE), lambda bi, c: (bi, 0, 0)),
                  pl.BlockSpec((LANES, SSD_WIDTH), lambda bi, c: (0, 0))],
        out_specs=[pl.BlockSpec((None, lc, SSD_WIDTH), lambda bi, c: (bi, c, 0)),
                   pl.BlockSpec((None, SSD_HEADS * SSD_HEAD_DIM, SSD_STATE), lambda bi, c: (bi, 0, 0))],
        out_shape=[jax.ShapeDtypeStruct((b, l, SSD_WIDTH), BF),
                   jax.ShapeDtypeStruct((b, SSD_HEADS * SSD_HEAD_DIM, SSD_STATE), F32)],
        scratch_shapes=[pltpu.VMEM((SSD_HEADS * SSD_HEAD_DIM, SSD_STATE), F32),
                        pltpu.VMEM((SUBLANES, CONV_DIM), F32),
                        pltpu.VMEM((SUBLANES + lc, CONV_DIM), F32)],
        compiler_params=_params("arbitrary", "arbitrary"),
        name="ssd",
    )(xbc3, z3, dtr3, conv_w, conv_b3, dtb3, alog3, dsk3, nw3, conv_st8, ssm_st3, expand)


def _outproj_kernel(a1_ref, a2_ref, w1_ref, w2_ref, x_ref, g_ref, o_ref, wb_ref):
    k1 = a1_ref.shape[1]

    @pl.when((pl.program_id(1) == 0) & (pl.program_id(2) == 0))
    def _():
        wb_ref[0:k1, :] = w1_ref[...].astype(BF)
        wb_ref[k1:, :] = w2_ref[...].astype(BF)

    acc = (jnp.dot(a1_ref[...], wb_ref[0:k1, :], preferred_element_type=F32)
           + jnp.dot(a2_ref[...], wb_ref[k1:, :], preferred_element_type=F32))
    o_ref[...] = x_ref[...] + g_ref[...] * acc


def _outproj(attn3, ssd3, w_out, layer, x3, mod, k_gate, tm, tn):
    b, l, k1 = attn3.shape
    k2 = ssd3.shape[2]
    per_tok = mod.shape[1] > 1
    nk = D_MODEL // tn
    mblk = (None, tm if per_tok else 1, tn)
    return pl.pallas_call(
        _outproj_kernel,
        grid=(nk, b, l // tm),
        in_specs=[pl.BlockSpec((None, tm, k1), lambda j, bi, i: (bi, i, 0)),
                  pl.BlockSpec((None, tm, k2), lambda j, bi, i: (bi, i, 0)),
                  pl.BlockSpec((None, k1, tn), lambda j, bi, i: (2 * layer, 0, j)),
                  pl.BlockSpec((None, k2, tn), lambda j, bi, i: (2 * layer + 1, 0, j)),
                  pl.BlockSpec((None, tm, tn), lambda j, bi, i: (bi, i, j)),
                  pl.BlockSpec(mblk, lambda j, bi, i: (bi, i if per_tok else 0, k_gate * nk + j))],
        out_specs=pl.BlockSpec((None, tm, tn), lambda j, bi, i: (bi, i, j)),
        out_shape=jax.ShapeDtypeStruct((b, l, D_MODEL), F32),
        scratch_shapes=[pltpu.VMEM((k1 + k2, tn), BF)],
        compiler_params=_params("arbitrary", "arbitrary", "arbitrary"),
        name="outproj",
    )(attn3, ssd3, w_out, w_out, x3, mod)


def _router_kernel(x_ref, w_ref, sc_ref, sh_ref, rw_ref, rb_ref, h_ref, idx_ref, gate_ref):
    x = x_ref[...]
    y = x * lax.rsqrt(jnp.mean(x * x, axis=-1, keepdims=True) + EPS) * w_ref[...]
    h = y * (1.0 + sc_ref[...]) + sh_ref[...]
    h_ref[...] = h
    logits = jnp.dot(h, rw_ref[...], precision=HI, preferred_element_type=F32) + rb_ref[...]
    lane = lax.broadcasted_iota(I32, logits.shape, 1)
    lane_f = lane.astype(F32)
    work = jnp.where(lane < N_EXPERTS, logits, -jnp.inf)
    idx_out = jnp.zeros(logits.shape, F32)
    val_out = jnp.zeros(logits.shape, F32)
    top = None
    for k in range(TOP_K):
        m = jnp.max(work, axis=-1, keepdims=True)
        sel = jnp.min(jnp.where(work == m, lane_f, float(LANES)), axis=-1, keepdims=True)
        if k == 0:
            top = m
        idx_out = jnp.where(lane == k, sel, idx_out)
        val_out = jnp.where(lane == k, jnp.exp(m - top), val_out)
        work = jnp.where(lane_f == sel, -jnp.inf, work)
    idx_ref[...] = idx_out.astype(I32)
    gate_ref[...] = val_out / jnp.sum(val_out, axis=-1, keepdims=True)


def _router(x3, norm_w3, layer, mod, k_sh, k_sc, router_wp, router_bp, tm):
    b, l, _ = x3.shape
    per_tok = mod.shape[1] > 1
    mblk = (None, tm if per_tok else 1, D_MODEL)

    def mspec(k):
        return pl.BlockSpec(mblk, lambda bi, i: (bi, i if per_tok else 0, k))

    return pl.pallas_call(
        _router_kernel,
        grid=(b, l // tm),
        in_specs=[pl.BlockSpec((None, tm, D_MODEL), lambda bi, i: (bi, i, 0)),
                  pl.BlockSpec((None, 1, D_MODEL), lambda bi, i: (layer, 0, 0)),
                  mspec(k_sc), mspec(k_sh),
                  pl.BlockSpec((None, D_MODEL, LANES), lambda bi, i: (layer, 0, 0)),
                  pl.BlockSpec((None, 1, LANES), lambda bi, i: (layer, 0, 0))],
        out_specs=[pl.BlockSpec((None, tm, D_MODEL), lambda bi, i: (bi, i, 0)),
                   pl.BlockSpec((None, tm, LANES), lambda bi, i: (bi, i, 0)),
                   pl.BlockSpec((None, tm, LANES), lambda bi, i: (bi, i, 0))],
        out_shape=[jax.ShapeDtypeStruct((b, l, D_MODEL), F32),
                   jax.ShapeDtypeStruct((b, l, LANES), I32),
                   jax.ShapeDtypeStruct((b, l, LANES), F32)],
        compiler_params=_params("arbitrary", "arbitrary"),
        name="router",
    )(x3, norm_w3, mod, mod, router_wp, router_bp)


def _row_copy(src_ref, src_row, dst_ref, dst_row, sem):
    return pltpu.make_async_copy(src_ref.at[pl.ds(src_row, 1)], dst_ref.at[pl.ds(dst_row, 1)], sem)


def _dispatch_kernel(nv_ref, tok_ref, hp_ref, hs_ref, o_ref, buf_ref, sem, *, n_prompt):
    t = buf_ref.shape[0]

    @pl.when(pl.program_id(0) < nv_ref[0])
    def _():
        def issue(r, carry):
            tok = tok_ref[0, r]

            @pl.when(tok < n_prompt)
            def _():
                _row_copy(hp_ref, tok, buf_ref, r, sem).start()

            @pl.when(tok >= n_prompt)
            def _():
                _row_copy(hs_ref, tok - n_prompt, buf_ref, r, sem).start()

            return carry

        lax.fori_loop(0, t, issue, 0)

        def drain(r, carry):
            _row_copy(hp_ref, 0, buf_ref, r, sem).wait()
            return carry

        lax.fori_loop(0, t, drain, 0)
        o_ref[...] = buf_ref[...].astype(o_ref.dtype)

    @pl.when(pl.program_id(0) >= nv_ref[0])
    def _():
        o_ref[...] = jnp.zeros(o_ref.shape, o_ref.dtype)


def _dispatch(n_valid, slot_tok3, h_prompt, h_sample_z):
    nt, _, t = slot_tok3.shape
    n_prompt = h_prompt.shape[0]
    grid_spec = pltpu.PrefetchScalarGridSpec(
        num_scalar_prefetch=1,
        grid=(nt,),
        in_specs=[pl.BlockSpec((None, 1, t), lambda i, nv: (jnp.minimum(i, nv[0] - 1), 0, 0),
                               memory_space=pltpu.SMEM),
                  pl.BlockSpec(memory_space=pl.ANY),
                  pl.BlockSpec(memory_space=pl.ANY)],
        out_specs=pl.BlockSpec((t, D_MODEL), lambda i, nv: (i, 0)),
        scratch_shapes=[pltpu.VMEM((t, D_MODEL), F32), pltpu.SemaphoreType.DMA(())],
    )
    return pl.pallas_call(
        functools.partial(_dispatch_kernel, n_prompt=n_prompt),
        grid_spec=grid_spec,
        out_shape=jax.ShapeDtypeStruct((nt * t, D_MODEL), BF),
        compiler_params=_params("arbitrary"),
        name="moe_dispatch",
    )(n_valid, slot_tok3, h_prompt, h_sample_z)


def _expert_changed(te_ref, i):
    return (i == 0) | (te_ref[i] != te_ref[jnp.maximum(i - 1, 0)])


def _moe_up_kernel(te_ref, nv_ref, x_ref, wg_ref, wu_ref, bg_ref, bu_ref, o_ref, wgb_ref, wub_ref):
    i = pl.program_id(1)
    valid = i < nv_ref[0]

    @pl.when(valid & _expert_changed(te_ref, i))
    def _():
        wgb_ref[...] = wg_ref[...].astype(BF)
        wub_ref[...] = wu_ref[...].astype(BF)

    @pl.when(valid)
    def _():
        x = x_ref[...]
        g = jnp.dot(x, wgb_ref[...], preferred_element_type=F32) + bg_ref[...]
        u = jnp.dot(x, wub_ref[...], preferred_element_type=F32) + bu_ref[...]
        g = jnp.minimum(g, SWIGLU_LIMIT)
        u = jnp.clip(u, -SWIGLU_LIMIT, SWIGLU_LIMIT)
        o_ref[...] = (g * jax.nn.sigmoid(SWIGLU_ALPHA * g) * (u + 1.0)).astype(o_ref.dtype)

    @pl.when(jnp.logical_not(valid))
    def _():
        o_ref[...] = jnp.zeros(o_ref.shape, o_ref.dtype)


def _moe_up(tile_e, n_valid, xs, w_gu, b_gu3, layer):
    ns = xs.shape[0]
    t, tn = MOE_TILE, MOE_UP_TN
    nt = ns // t
    nj = EXPERT_DIM // tn
    e0 = layer * N_EXPERTS

    def row(i, nv):
        return jnp.minimum(i, nv[0] - 1)

    grid_spec = pltpu.PrefetchScalarGridSpec(
        num_scalar_prefetch=2,
        grid=(nj, nt),
        in_specs=[pl.BlockSpec((t, D_MODEL), lambda j, i, te, nv: (row(i, nv), 0)),
                  pl.BlockSpec((None, D_MODEL, tn), lambda j, i, te, nv: (e0 + te[i], 0, j)),
                  pl.BlockSpec((None, D_MODEL, tn), lambda j, i, te, nv: (e0 + te[i], 0, nj + j)),
                  pl.BlockSpec((None, 1, tn), lambda j, i, te, nv: (e0 + te[i], 0, j)),
                  pl.BlockSpec((None, 1, tn), lambda j, i, te, nv: (e0 + te[i], 0, nj + j))],
        out_specs=pl.BlockSpec((t, tn), lambda j, i, te, nv: (i, j)),
        scratch_shapes=[pltpu.VMEM((D_MODEL, tn), BF), pltpu.VMEM((D_MODEL, tn), BF)],
    )
    return pl.pallas_call(
        _moe_up_kernel,
        grid_spec=grid_spec,
        out_shape=jax.ShapeDtypeStruct((ns, EXPERT_DIM), BF),
        compiler_params=_params("arbitrary", "arbitrary"),
        name="moe_up",
    )(tile_e, n_valid, xs, w_gu, w_gu, b_gu3, b_gu3)


def _moe_down_kernel(te_ref, nv_ref, a_ref, w_ref, b_ref, o_ref, wb_ref):
    i = pl.program_id(1)
    valid = i < nv_ref[0]

    @pl.when(valid & _expert_changed(te_ref, i))
    def _():
        wb_ref[...] = w_ref[...].astype(BF)

    @pl.when(valid)
    def _():
        o_ref[...] = jnp.dot(a_ref[...], wb_ref[...], preferred_element_type=F32) + b_ref[...]

    @pl.when(jnp.logical_not(valid))
    def _():
        o_ref[...] = jnp.zeros(o_ref.shape, o_ref.dtype)


def _moe_down(tile_e, n_valid, act, w_down, b_down3, layer):
    ns = act.shape[0]
    t, tn = MOE_TILE, MOE_DOWN_TN
    nt = ns // t
    nj = D_MODEL // tn
    e0 = layer * N_EXPERTS

    def row(i, nv):
        return jnp.minimum(i, nv[0] - 1)

    grid_spec = pltpu.PrefetchScalarGridSpec(
        num_scalar_prefetch=2,
        grid=(nj, nt),
        in_specs=[pl.BlockSpec((t, EXPERT_DIM), lambda j, i, te, nv: (row(i, nv), 0)),
                  pl.BlockSpec((None, EXPERT_DIM, tn), lambda j, i, te, nv: (e0 + te[i], 0, j)),
                  pl.BlockSpec((None, 1, tn), lambda j, i, te, nv: (e0 + te[i], 0, j))],
        out_specs=pl.BlockSpec((t, tn), lambda j, i, te, nv: (i, j)),
        scratch_shapes=[pltpu.VMEM((EXPERT_DIM, tn), BF)],
    )
    return pl.pallas_call(
        _moe_down_kernel,
        grid_spec=grid_spec,
        out_shape=jax.ShapeDtypeStruct((ns, D_MODEL), F32),
        compiler_params=_params("arbitrary", "arbitrary"),
        name="moe_down",
    )(tile_e, n_valid, act, w_down, b_down3)


def _combine_kernel(slot_ref, y_ref, gate_ref, x_ref, g_ref, o_ref, buf_ref, sem):
    tt = x_ref.shape[0]

    def issue(r, carry):
        for k in range(TOP_K):
            _row_copy(y_ref, slot_ref[0, r * TOP_K + k], buf_ref.at[k], r, sem).start()
        return carry

    lax.fori_loop(0, tt, issue, 0)

    def drain(r, carry):
        for k in range(TOP_K):
            _row_copy(y_ref, 0, buf_ref.at[k], r, sem).wait()
        return carry

    lax.fori_loop(0, tt, drain, 0)
    gates = gate_ref[...]
    acc = gates[:, 0:1] * buf_ref[0]
    for k in range(1, TOP_K):
        acc = acc + gates[:, k:k + 1] * buf_ref[k]
    o_ref[...] = x_ref[...] + g_ref[...] * acc


def _combine(slots3, y_slots, gates3, x3, mod, k_gate, tt):
    b, l, _ = x3.shape
    per_tok = mod.shape[1] > 1
    lt = l // tt
    mblk = (None, tt if per_tok else 1, D_MODEL)
    return pl.pallas_call(
        _combine_kernel,
        grid=(b, lt),
        in_specs=[pl.BlockSpec((None, 1, tt * TOP_K), lambda bi, i: (bi * lt + i, 0, 0), memory_space=pltpu.SMEM),
                  pl.BlockSpec(memory_space=pl.ANY),
                  pl.BlockSpec((None, tt, LANES), lambda bi, i: (bi, i, 0)),
                  pl.BlockSpec((None, tt, D_MODEL), lambda bi, i: (bi, i, 0)),
                  pl.BlockSpec(mblk, lambda bi, i: (bi, i if per_tok else 0, k_gate))],
        out_specs=pl.BlockSpec((None, tt, D_MODEL), lambda bi, i: (bi, i, 0)),
        out_shape=jax.ShapeDtypeStruct((b, l, D_MODEL), F32),
        scratch_shapes=[pltpu.VMEM((TOP_K, tt, D_MODEL), F32), pltpu.SemaphoreType.DMA(())],
        compiler_params=_params("arbitrary", "arbitrary"),
        name="moe_combine",
    )(slots3, y_slots, gates3, x3, mod)


def _routing_tables(top_idx, n_tokens):
    m = n_tokens * TOP_K
    t = MOE_TILE
    flat_e = top_idx.reshape(m)
    onehot = (flat_e[:, None] == jnp.arange(N_EXPERTS, dtype=I32)[None, :]).astype(I32)
    csum = jnp.cumsum(onehot, axis=0)
    pos = jnp.sum(csum * onehot, axis=1) - 1
    counts = csum[-1]
    tiles_e = (counts + t - 1) // t
    tile_end = jnp.cumsum(tiles_e)
    tile_start = tile_end - tiles_e
    slot = tile_start[flat_e] * t + pos
    n_tiles = m // t + N_EXPERTS
    n_valid = tile_end[-1]
    tile_ids = jnp.minimum(jnp.arange(n_tiles, dtype=I32), n_valid - 1)
    tile_e = jnp.minimum(jnp.searchsorted(tile_end, tile_ids, side='right'), N_EXPERTS - 1).astype(I32)
    flat_tok = jnp.arange(m, dtype=I32) // TOP_K
    slot_tok = jnp.full((n_tiles * t,), n_tokens, I32).at[slot].set(flat_tok)
    return slot.astype(I32), slot_tok.reshape(n_tiles, 1, t), tile_e, n_valid.astype(I32).reshape(1)


def _mixing(xr, mod, b, l, pos, k_past, v_past, conv_state, ssm_state, weights, layer, tm):
    (norm_mix_w3, w_in, w_dt3, q_norm_w3, k_norm_w3, ssd_consts, w_out) = weights
    m = b * l
    h = _norm_mod(xr, norm_mix_w3, layer, mod, 0, 1, tm).reshape(m, D_MODEL)
    qkv = _matmul(h, w_in, layer, 0, QKV_WIDTH // 512, 512, tm)
    z = _matmul(h, w_in, layer, QKV_WIDTH // 512, SSD_WIDTH // 512, 512, tm)
    xbc = _matmul(h, w_in, layer, (QKV_WIDTH + SSD_WIDTH) // 512, CONV_DIM // 512, 512, tm)
    dtr = _matmul(h, w_dt3, layer, 0, 1, LANES, tm)

    cosf, sinf = _rope_tables(jnp.tile(pos, b))
    qn, kn = _qk_norm_rope(qkv, cosf, sinf, q_norm_w3, k_norm_w3, layer, tm)
    v = qkv[:, ATTN_WIDTH + KV_WIDTH:]

    if k_past is None:
        attn3 = _attn_prompt(qn.reshape(b, l, ATTN_WIDTH), kn.reshape(b, l, KV_WIDTH),
                             qkv.reshape(b, l, QKV_WIDTH))
        xbc3, z3, dtr3 = (a.reshape(b, l, -1) for a in (xbc, z, dtr))
        valid = SSD_CHUNK
    else:
        rows = Q_PER_KV * l
        qg = qn.reshape(b, l, KV_HEADS, Q_PER_KV, HEAD_DIM).transpose(0, 2, 3, 1, 4).reshape(b, KV_HEADS, rows, HEAD_DIM)
        padn = ((0, 0), (0, LANES - l), (0, 0))
        og = _attn_sample(qg, k_past, v_past, jnp.pad(kn.reshape(b, l, KV_WIDTH), padn),
                          jnp.pad(v.reshape(b, l, KV_WIDTH), padn), l)
        attn3 = og.reshape(b, KV_HEADS, Q_PER_KV, l, HEAD_DIM).transpose(0, 3, 1, 2, 4).reshape(b, l, ATTN_WIDTH)
        attn3 = attn3.astype(BF)
        padc = ((0, 0), (0, SSD_CHUNK - l), (0, 0))
        xbc3, z3, dtr3 = (jnp.pad(a.reshape(b, l, -1), padc) for a in (xbc, z, dtr))
        valid = l

    conv_st8 = jnp.pad(conv_state, ((0, 0), (SUBLANES - (CONV_WIDTH - 1), 0), (0, 0)))
    ssd3, h_fin = _ssd(xbc3, z3, dtr3, conv_st8, ssm_state.reshape(b, SSD_HEADS * SSD_HEAD_DIM, SSD_STATE),
                       ssd_consts, layer, valid)
    ssd3 = ssd3[:, :l]
    x1 = _outproj(attn3.reshape(xr.shape[0], xr.shape[1], ATTN_WIDTH), ssd3.reshape(xr.shape[0], xr.shape[1], SSD_WIDTH),
                  w_out, layer, xr, mod, 2, tm, 512)
    keep = CONV_WIDTH - 1
    assert l >= keep
    new_conv = xbc.reshape(b, l, CONV_DIM)[:, l - keep:]
    return (x1, kn.reshape(b, l, KV_HEADS, HEAD_DIM), v.reshape(b, l, KV_HEADS, HEAD_DIM), new_conv,
            h_fin.reshape(b, SSD_HEADS, SSD_HEAD_DIM, SSD_STATE))


def kernel(x_prompt, x_sample, cache_k, cache_v, state_conv, state_ssm, c_prompt, c_sample, w_ada, b_ada, norm_mix_w, norm_ffn_w, w_in, q_norm_w, k_norm_w, conv_w, conv_b, dt_bias, a_log, d_skip, ssd_norm_w, w_out, router_w, router_b, w_gate_up, b_gate_up, w_down, b_down):
    depth = w_ada.shape[0]
    bp, seq, _ = x_prompt.shape
    bs, dseq, _ = x_sample.shape
    kv_win = cache_k.shape[2]
    assert kv_win == MAX_REACH and PAST_LEN >= MAX_REACH
    n_p, n_s = bp * seq, bs * dseq
    n_tok = n_p + n_s
    keep_p = min(MAX_REACH, seq)

    vec3 = lambda a: a.reshape(depth, 1, -1)
    pad_lanes = lambda a: jnp.pad(a, [(0, 0)] * (a.ndim - 1) + [(0, LANES - a.shape[-1])])
    b_ada3 = vec3(b_ada)
    norm_mix_w3, norm_ffn_w3 = vec3(norm_mix_w), vec3(norm_ffn_w)
    w_dt3 = pad_lanes(w_in[:, :, DT_COL0:])
    expand = (jnp.arange(LANES, dtype=I32)[:, None]
              == jnp.arange(SSD_WIDTH, dtype=I32)[None, :] // SSD_HEAD_DIM).astype(F32)
    ssd_consts = (conv_w, vec3(conv_b), vec3(pad_lanes(dt_bias)), vec3(pad_lanes(a_log)),
                  vec3(jnp.repeat(d_skip, SSD_HEAD_DIM, axis=-1)), vec3(ssd_norm_w), expand)
    weights = (norm_mix_w3, w_in, w_dt3, vec3(q_norm_w), vec3(k_norm_w), ssd_consts,
               w_out.reshape(depth, 2, MIX_WIDTH // 2, D_MODEL).reshape(depth * 2, MIX_WIDTH // 2, D_MODEL))
    router_wp = pad_lanes(router_w)
    router_bp = vec3(pad_lanes(router_b))
    w_gu = w_gate_up.reshape(depth * N_EXPERTS, D_MODEL, 2 * EXPERT_DIM)
    b_gu3 = b_gate_up.reshape(depth * N_EXPERTS, 1, 2 * EXPERT_DIM)
    w_dn = w_down.reshape(depth * N_EXPERTS, EXPERT_DIM, D_MODEL)
    b_dn3 = b_down.reshape(depth * N_EXPERTS, 1, D_MODEL)

    n_c = bp + bs
    c_all = jnp.concatenate([c_prompt, c_sample, jnp.zeros((-n_c % SUBLANES, D_MODEL), F32)], axis=0)
    pos_p = jnp.arange(seq, dtype=I32)
    pos_s = PAST_LEN + jnp.arange(dseq, dtype=I32)

    y_p = x_prompt
    y_s = x_sample.reshape(1, n_s, D_MODEL)
    outs = [[] for _ in range(8)]
    for layer in range(depth):
        mod = _ada(c_all, w_ada, b_ada3, layer)
        mod_p = mod[:bp].reshape(bp, 1, 6 * D_MODEL)
        mod_s = jnp.repeat(mod[bp:n_c], dseq, axis=0).reshape(1, n_s, 6 * D_MODEL)

        zero_conv = jnp.zeros((bp, CONV_WIDTH - 1, CONV_DIM), F32)
        zero_ssm = jnp.zeros((bp, SSD_HEADS, SSD_HEAD_DIM, SSD_STATE), F32)
        x1_p, kp, vp, cp, sp = _mixing(y_p, mod_p, bp, seq, pos_p, None, None, zero_conv, zero_ssm,
                                       weights, layer, 512)
        x1_s, kn, vn, cn, sn = _mixing(y_s, mod_s, bs, dseq, pos_s, cache_k[layer].reshape(bs, kv_win, KV_WIDTH),
                                       cache_v[layer].reshape(bs, kv_win, KV_WIDTH), state_conv[layer],
                                       state_ssm[layer], weights, layer, n_s)

        h2_p, idx_p, gate_p = _router(x1_p, norm_ffn_w3, layer, mod_p, 3, 4, router_wp, router_bp, 256)
        h2_s, idx_s, gate_s = _router(x1_s, norm_ffn_w3, layer, mod_s, 3, 4, router_wp, router_bp, n_s)
        top_idx = jnp.concatenate([idx_p.reshape(n_p, LANES)[:, :TOP_K], idx_s.reshape(n_s, LANES)[:, :TOP_K]], axis=0)
        slot, slot_tok3, tile_e, n_valid = _routing_tables(top_idx, n_tok)
        h2_sz = jnp.concatenate([h2_s.reshape(n_s, D_MODEL), jnp.zeros((SUBLANES, D_MODEL), F32)], axis=0)
        xs = _dispatch(n_valid, slot_tok3, h2_p.reshape(n_p, D_MODEL), h2_sz)
        act = _moe_up(tile_e, n_valid, xs, w_gu, b_gu3, layer)
        y_slots = _moe_down(tile_e, n_valid, act, w_dn, b_dn3, layer)
        tt = 128
        slots_p = slot[:n_p * TOP_K].reshape(n_p // tt, 1, tt * TOP_K)
        slots_s = slot[n_p * TOP_K:].reshape(n_s // tt, 1, tt * TOP_K)
        y_p = _combine(slots_p, y_slots, gate_p, x1_p, mod_p, 5, tt)
        y_s = _combine(slots_s, y_slots, gate_s, x1_s, mod_s, 5, tt)

        for lst, val in zip(outs, (kp[:, seq - keep_p:], vp[:, seq - keep_p:], cp, sp, kn, vn, cn, sn)):
            lst.append(val)
    return (y_p, y_s.reshape(bs, dseq, D_MODEL)) + tuple(jnp.stack(o) for o in outs)
```

```python
import functools
import math

import jax
import jax.numpy as jnp
from jax import lax
from jax.experimental import pallas as pl
from jax.experimental.pallas import tpu as pltpu

F32 = jnp.float32
BF = jnp.bfloat16
I32 = jnp.int32

D_MODEL = 4096
PAST_LEN = 8192
HEAD_DIM = 128
ATTN_HEADS = 16
KV_HEADS = 4
Q_PER_KV = ATTN_HEADS // KV_HEADS
ATTN_WIDTH = ATTN_HEADS * HEAD_DIM
KV_WIDTH = KV_HEADS * HEAD_DIM
ROT_DIM = HEAD_DIM // 4
ROPE_THETA = 500000.0
DILATED_BRANCHES = ((128, 1), (512, 4), (2048, 16))
MAX_REACH = 2048
SSD_WIDTH = 2048
SSD_HEAD_DIM = 64
SSD_HEADS = 32
SSD_GROUPS = 8
SSD_STATE = 128
CONV_WIDTH = 4
CONV_DIM = SSD_WIDTH + 2 * SSD_GROUPS * SSD_STATE
SSD_CHUNK = 128
MIX_WIDTH = ATTN_WIDTH + SSD_WIDTH
N_EXPERTS = 32
TOP_K = 4
EXPERT_DIM = D_MODEL
SWIGLU_LIMIT = 7.0
SWIGLU_ALPHA = 1.702
EPS = 1e-6

LANES = 128
SUBLANES = 8
NEG = -1e30
QKV_WIDTH = ATTN_WIDTH + 2 * KV_WIDTH
DT_COL0 = QKV_WIDTH + SSD_WIDTH + CONV_DIM
MOE_TILE = 256
MOE_UP_TN = 512
MOE_DOWN_TN = 1024
VMEM_LIMIT = 56 * 2 ** 20
HI = lax.Precision.HIGHEST
NT_DIMS = (((1,), (1,)), ((), ()))
TN_DIMS = (((0,), (0,)), ((), ()))


def _params(*sem):
    return pltpu.CompilerParams(dimension_semantics=sem, vmem_limit_bytes=VMEM_LIMIT)


def _silu(x):
    return x * jax.nn.sigmoid(x)


def _ada_kernel(c_ref, w_ref, b_ref, o_ref):
    a = _silu(c_ref[...]).astype(BF)
    o_ref[...] = jnp.dot(a, w_ref[...].astype(BF), preferred_element_type=F32) + b_ref[...]


def _ada(c, w_ada, b_ada3, layer):
    mc = c.shape[0]
    n = 6 * D_MODEL
    tn = 512
    return pl.pallas_call(
        _ada_kernel,
        grid=(n // tn,),
        in_specs=[pl.BlockSpec((mc, D_MODEL), lambda j: (0, 0)),
                  pl.BlockSpec((None, D_MODEL, tn), lambda j: (layer, 0, j)),
                  pl.BlockSpec((None, 1, tn), lambda j: (layer, 0, j))],
        out_specs=pl.BlockSpec((mc, tn), lambda j: (0, j)),
        out_shape=jax.ShapeDtypeStruct((mc, n), F32),
        compiler_params=_params("arbitrary"),
        name="ada",
    )(c, w_ada, b_ada3)


def _norm_mod_kernel(x_ref, w_ref, sc_ref, sh_ref, o_ref):
    x = x_ref[...]
    y = x * lax.rsqrt(jnp.mean(x * x, axis=-1, keepdims=True) + EPS) * w_ref[...]
    o_ref[...] = (y * (1.0 + sc_ref[...]) + sh_ref[...]).astype(o_ref.dtype)


def _norm_mod(x3, norm_w3, layer, mod, k_sh, k_sc, tm):
    b, l, _ = x3.shape
    per_tok = mod.shape[1] > 1
    mblk = (None, tm if per_tok else 1, D_MODEL)

    def mspec(k):
        return pl.BlockSpec(mblk, lambda bi, i: (bi, i if per_tok else 0, k))

    return pl.pallas_call(
        _norm_mod_kernel,
        grid=(b, l // tm),
        in_specs=[pl.BlockSpec((None, tm, D_MODEL), lambda bi, i: (bi, i, 0)),
                  pl.BlockSpec((None, 1, D_MODEL), lambda bi, i: (layer, 0, 0)),
                  mspec(k_sc), mspec(k_sh)],
        out_specs=pl.BlockSpec((None, tm, D_MODEL), lambda bi, i: (bi, i, 0)),
        out_shape=jax.ShapeDtypeStruct((b, l, D_MODEL), BF),
        compiler_params=_params("arbitrary", "arbitrary"),
        name="norm_mod",
    )(x3, norm_w3, mod, mod)


def _mm_kernel(a_ref, w_ref, o_ref, wb_ref):
    @pl.when(pl.program_id(1) == 0)
    def _():
        wb_ref[...] = w_ref[...].astype(BF)

    o_ref[...] = jnp.dot(a_ref[...], wb_ref[...], preferred_element_type=F32)


def _matmul(a, w3, layer, col_blk0, n_tiles, tn, tm):
    m, k = a.shape
    return pl.pallas_call(
        _mm_kernel,
        grid=(n_tiles, m // tm),
        in_specs=[pl.BlockSpec((tm, k), lambda j, i: (i, 0)),
                  pl.BlockSpec((None, k, tn), lambda j, i: (layer, 0, col_blk0 + j))],
        out_specs=pl.BlockSpec((tm, tn), lambda j, i: (i, j)),
        out_shape=jax.ShapeDtypeStruct((m, n_tiles * tn), F32),
        scratch_shapes=[pltpu.VMEM((k, tn), BF)],
        compiler_params=_params("arbitrary", "arbitrary"),
        name="matmul",
    )(a, w3)


def _rope_tables(pos):
    half = ROT_DIM // 2
    inv_freq = jnp.power(ROPE_THETA, -jnp.arange(half, dtype=F32) * 2.0 / ROT_DIM)
    ang = pos.astype(F32)[:, None] * inv_freq[None, :]
    cos, sin = jnp.cos(ang), jnp.sin(ang)
    n = pos.shape[0]
    cosf = jnp.concatenate([cos, cos, jnp.ones((n, HEAD_DIM - ROT_DIM), F32)], axis=1)
    sinf = jnp.concatenate([-sin, sin, jnp.zeros((n, HEAD_DIM - ROT_DIM), F32)], axis=1)
    return cosf, sinf


def _norm_rope_head(xh, w, cosf, sinf, lane):
    half = ROT_DIM // 2
    y = xh * lax.rsqrt(jnp.mean(xh * xh, axis=-1, keepdims=True) + EPS) * w
    partner = jnp.where(lane < half, pltpu.roll(y, HEAD_DIM - half, 1), pltpu.roll(y, half, 1))
    return y * cosf + partner * sinf


def _qk_kernel(q_ref, k_ref, cos_ref, sin_ref, qw_ref, kw_ref, qo_ref, ko_ref):
    cosf = cos_ref[...]
    sinf = sin_ref[...]
    lane = lax.broadcasted_iota(I32, cosf.shape, 1)
    scale = HEAD_DIM ** -0.5
    for h in range(ATTN_HEADS):
        sl = slice(h * HEAD_DIM, (h + 1) * HEAD_DIM)
        qo_ref[:, sl] = _norm_rope_head(q_ref[:, sl], qw_ref[...], cosf, sinf, lane) * scale
    for h in range(KV_HEADS):
        sl = slice(h * HEAD_DIM, (h + 1) * HEAD_DIM)
        ko_ref[:, sl] = _norm_rope_head(k_ref[:, sl], kw_ref[...], cosf, sinf, lane)


def _qk_norm_rope(qkv, cosf, sinf, q_norm_w3, k_norm_w3, layer, tm):
    m = qkv.shape[0]
    return pl.pallas_call(
        _qk_kernel,
        grid=(m // tm,),
        in_specs=[pl.BlockSpec((tm, ATTN_WIDTH), lambda i: (i, 0)),
                  pl.BlockSpec((tm, KV_WIDTH), lambda i: (i, ATTN_WIDTH // KV_WIDTH)),
                  pl.BlockSpec((tm, HEAD_DIM), lambda i: (i, 0)),
                  pl.BlockSpec((tm, HEAD_DIM), lambda i: (i, 0)),
                  pl.BlockSpec((None, 1, HEAD_DIM), lambda i: (layer, 0, 0)),
                  pl.BlockSpec((None, 1, HEAD_DIM), lambda i: (layer, 0, 0))],
        out_specs=[pl.BlockSpec((tm, ATTN_WIDTH), lambda i: (i, 0)),
                   pl.BlockSpec((tm, KV_WIDTH), lambda i: (i, 0))],
        out_shape=[jax.ShapeDtypeStruct((m, ATTN_WIDTH), F32),
                   jax.ShapeDtypeStruct((m, KV_WIDTH), F32)],
        compiler_params=_params("arbitrary"),
        name="qk_norm_rope",
    )(qkv, qkv, cosf, sinf, q_norm_w3, k_norm_w3)


def _multiplicity(delta):
    ge0 = delta >= 0
    cnt = jnp.zeros(delta.shape, F32)
    for window, dil in DILATED_BRANCHES:
        hit = ge0 & (delta <= window) & ((delta & (dil - 1)) == 0)
        cnt = cnt + jnp.where(hit, 1.0, 0.0)
    return cnt


ATTN_QB = 256
ATTN_KB = 128


def _attn_prompt_kernel(q_ref, k_ref, v_ref, tab_ref, o_ref):
    c = pl.program_id(2)
    qb = q_ref.shape[0]
    kb = tab_ref.shape[1]
    per = qb // kb
    q = q_ref[...]
    qs = jnp.concatenate([q[:, h * HEAD_DIM:(h + 1) * HEAD_DIM] for h in range(Q_PER_KV)], axis=0).astype(BF)
    last_kb = per * c + per - 1
    n_it = jnp.minimum(last_kb + 1, tab_ref.shape[0])

    def body(t, carry):
        m, l, acc = carry
        start = pl.multiple_of((last_kb - t) * kb, kb)
        kblk = k_ref[pl.ds(start, kb), :].astype(BF)
        vblk = v_ref[pl.ds(start, kb), :].astype(BF)
        cnt = jnp.concatenate([tab_ref[t]] * Q_PER_KV, axis=1)
        s = lax.dot_general(kblk, qs, NT_DIMS, preferred_element_type=F32)
        sm = jnp.where(cnt > 0.0, s, NEG)
        m_new = jnp.maximum(m, jnp.max(sm, axis=0, keepdims=True))
        p = cnt * jnp.exp(sm - m_new)
        alpha = jnp.exp(m - m_new)
        l = alpha * l + jnp.sum(p, axis=0, keepdims=True)
        acc = alpha * acc + lax.dot_general(vblk, p.astype(BF), TN_DIMS, preferred_element_type=F32)
        return m_new, l, acc

    cols = Q_PER_KV * qb
    init = (jnp.full((1, cols), NEG, F32), jnp.zeros((1, cols), F32), jnp.zeros((HEAD_DIM, cols), F32))
    _, l, acc = lax.fori_loop(0, n_it, body, init)
    out_t = acc / l
    for h in range(Q_PER_KV):
        o_ref[:, h * HEAD_DIM:(h + 1) * HEAD_DIM] = out_t[:, h * qb:(h + 1) * qb].T.astype(o_ref.dtype)


def _attn_prompt(qn3, kn3, qkv3):
    b, l, _ = qn3.shape
    qb, kb = ATTN_QB, ATTN_KB
    assert l % qb == 0
    gw = Q_PER_KV * HEAD_DIM
    v_blk0 = (ATTN_WIDTH + KV_WIDTH) // HEAD_DIM
    n_tab = MAX_REACH // kb + qb // kb
    tt = jnp.arange(n_tab, dtype=I32)[:, None, None]
    jj = jnp.arange(kb, dtype=I32)[None, :, None]
    ii = jnp.arange(qb, dtype=I32)[None, None, :]
    tables = _multiplicity((tt - (qb // kb - 1)) * kb + ii - jj)
    return pl.pallas_call(
        _attn_prompt_kernel,
        grid=(b, KV_HEADS, l // qb),
        in_specs=[pl.BlockSpec((None, qb, gw), lambda bi, g, c: (bi, c, g)),
                  pl.BlockSpec((None, l, HEAD_DIM), lambda bi, g, c: (bi, 0, g)),
                  pl.BlockSpec((None, l, HEAD_DIM), lambda bi, g, c: (bi, 0, v_blk0 + g)),
                  pl.BlockSpec((n_tab, kb, qb), lambda bi, g, c: (0, 0, 0))],
        out_specs=pl.BlockSpec((None, qb, gw), lambda bi, g, c: (bi, c, g)),
        out_shape=jax.ShapeDtypeStruct((b, l, ATTN_WIDTH), BF),
        compiler_params=_params("arbitrary", "arbitrary", "arbitrary"),
        name="attn_prompt",
    )(qn3, kn3, qkv3, tables)


def _attn_sample_kernel(q_ref, ck_ref, cv_ref, kn_ref, vn_ref, o_ref, *, n_new):
    w = ck_ref.shape[0] // KV_HEADS
    pad = kn_ref.shape[0]
    rows = q_ref.shape[1]
    ri = lax.broadcasted_iota(I32, (rows, w + pad), 0) & (n_new - 1)
    cj = lax.broadcasted_iota(I32, (rows, w + pad), 1)
    cnt = _multiplicity(w + ri - cj)
    for g in range(KV_HEADS):
        sl = slice(g * HEAD_DIM, (g + 1) * HEAD_DIM)
        qg = q_ref[g].astype(BF)
        s_c = lax.dot_general(qg, ck_ref[pl.ds(g, w, stride=KV_HEADS), :].astype(BF), NT_DIMS,
                              preferred_element_type=F32)
        s_n = lax.dot_general(qg, kn_ref[:, sl].astype(BF), NT_DIMS, preferred_element_type=F32)
        s = jnp.concatenate([s_c, s_n], axis=1)
        sm = jnp.where(cnt > 0.0, s, NEG)
        m = jnp.max(sm, axis=-1, keepdims=True)
        p = cnt * jnp.exp(sm - m)
        l = jnp.sum(p, axis=-1, keepdims=True)
        pb = p.astype(BF)
        o = (jnp.dot(pb[:, :w], cv_ref[pl.ds(g, w, stride=KV_HEADS), :].astype(BF), preferred_element_type=F32)
             + jnp.dot(pb[:, w:], vn_ref[:, sl].astype(BF), preferred_element_type=F32))
        o_ref[g] = o / l


def _attn_sample(qg4, cache_k, cache_v, layer, knp3, vnp3, n_new):
    b, _, rows, _ = qg4.shape
    depth, _, w = cache_k.shape[:3]
    pad = knp3.shape[1]
    cache_k, cache_v = (a.reshape(depth, b, w * KV_HEADS, HEAD_DIM) for a in (cache_k, cache_v))
    cache_spec = pl.BlockSpec((None, None, w * KV_HEADS, HEAD_DIM), lambda i: (layer, i, 0, 0))
    return pl.pallas_call(
        functools.partial(_attn_sample_kernel, n_new=n_new),
        grid=(b,),
        in_specs=[pl.BlockSpec((None, KV_HEADS, rows, HEAD_DIM), lambda i: (i, 0, 0, 0)),
                  cache_spec, cache_spec,
                  pl.BlockSpec((None, pad, KV_WIDTH), lambda i: (i, 0, 0)),
                  pl.BlockSpec((None, pad, KV_WIDTH), lambda i: (i, 0, 0))],
        out_specs=pl.BlockSpec((None, KV_HEADS, rows, HEAD_DIM), lambda i: (i, 0, 0, 0)),
        out_shape=jax.ShapeDtypeStruct((b, KV_HEADS, rows, HEAD_DIM), F32),
        compiler_params=_params("arbitrary"),
        name="attn_sample",
    )(qg4, cache_k, cache_v, knp3, vnp3)


def _ssd_kernel(x_ref, z_ref, dtr_ref, cw_ref, cb_ref, dtb_ref, alog_ref, dsk_ref, nw_ref, cst_ref, sst_ref,
                e_ref, y_ref, hout_ref, h_ref, tail_ref, buf_ref, *, valid):
    lc = x_ref.shape[0]
    c = pl.program_id(1)
    gw = SSD_WIDTH // SSD_GROUPS
    hpg = SSD_HEADS // SSD_GROUPS
    b_col0 = SSD_WIDTH
    c_col0 = SSD_WIDTH + SSD_GROUPS * SSD_STATE

    @pl.when(c == 0)
    def _():
        h_ref[...] = sst_ref[...]
        tail_ref[...] = cst_ref[...]

    x = x_ref[...]
    buf_ref[0:SUBLANES, :] = tail_ref[...]
    buf_ref[SUBLANES:SUBLANES + lc, :] = x
    tail_ref[...] = x[lc - SUBLANES:lc, :]
    conv = cb_ref[...]
    for j in range(CONV_WIDTH):
        r0 = SUBLANES - (CONV_WIDTH - 1) + j
        conv = conv + cw_ref[j:j + 1, :] * buf_ref[r0:r0 + lc, :]
    u = _silu(conv)
    xs = u[:, :SSD_WIDTH]

    dtv = dtr_ref[...] + dtb_ref[...]
    dt = jnp.maximum(dtv, 0.0) + jnp.log1p(jnp.exp(-jnp.abs(dtv)))
    row = lax.broadcasted_iota(I32, (lc, lc), 0)
    col = lax.broadcasted_iota(I32, (lc, lc), 1)
    if valid < lc:
        dt = jnp.where(lax.broadcasted_iota(I32, dt.shape, 0) < valid, dt, 0.0)
    da = dt * (-jnp.exp(alog_ref[...]))
    causal = row >= col
    acs = jnp.dot(jnp.where(causal, 1.0, 0.0), da, precision=HI, preferred_element_type=F32)
    acs_t = acs.T
    last = acs[lc - 1:lc, :]
    expand = e_ref[...]
    dt_e = jnp.dot(dt, expand, precision=HI, preferred_element_type=F32)
    dfs_e = jnp.dot(jnp.exp(acs), expand, precision=HI, preferred_element_type=F32)
    dte_e = jnp.dot(jnp.exp(last - acs), expand, precision=HI, preferred_element_type=F32)
    xdt = xs * dt_e
    xdtw = xdt * dte_e
    lane = lax.broadcasted_iota(I32, (lc, LANES), 1)
    z = z_ref[...]

    for g in range(SSD_GROUPS):
        gs = slice(g * gw, (g + 1) * gw)
        bg = u[:, b_col0 + g * SSD_STATE:b_col0 + (g + 1) * SSD_STATE].astype(BF)
        cg = u[:, c_col0 + g * SSD_STATE:c_col0 + (g + 1) * SSD_STATE].astype(BF)
        cb = lax.dot_general(cg, bg, NT_DIMS, preferred_element_type=F32)
        hg = h_ref[g * gw:(g + 1) * gw, :]
        y_off = lax.dot_general(cg, hg.astype(BF), NT_DIMS, preferred_element_type=F32)
        pieces = []
        for pr in range(hpg // 2):
            xp = xdt[:, g * gw + pr * LANES:g * gw + (pr + 1) * LANES]
            acc = jnp.zeros((lc, LANES), F32)
            for hh in range(2):
                h = g * hpg + pr * 2 + hh
                diff = acs[:, h:h + 1] - acs_t[h:h + 1, :]
                decay = jnp.exp(jnp.where(causal, diff, -jnp.inf))
                mat = (cb * decay).astype(BF)
                keep = (lane < SSD_HEAD_DIM) if hh == 0 else (lane >= SSD_HEAD_DIM)
                acc = acc + jnp.dot(mat, jnp.where(keep, xp, 0.0).astype(BF), preferred_element_type=F32)
            pieces.append(acc)
        y_diag = jnp.concatenate(pieces, axis=1)
        yg = y_diag + y_off * dfs_e[:, gs] + xs[:, gs] * dsk_ref[:, gs]

        ds = lax.dot_general(xdtw[:, gs].astype(BF), bg, TN_DIMS, preferred_element_type=F32)
        for hh in range(hpg):
            h = g * hpg + hh
            hs = slice(hh * SSD_HEAD_DIM, (hh + 1) * SSD_HEAD_DIM)
            h_ref[g * gw + hh * SSD_HEAD_DIM:g * gw + (hh + 1) * SSD_HEAD_DIM, :] = (
                hg[hs] * jnp.exp(last[:, h:h + 1]) + ds[hs])

        yg = yg * _silu(z[:, gs])
        yg = yg * lax.rsqrt(jnp.mean(yg * yg, axis=-1, keepdims=True) + EPS)
        y_ref[:, gs] = (yg * nw_ref[:, gs]).astype(y_ref.dtype)

    @pl.when(c == pl.num_programs(1) - 1)
    def _():
        hout_ref[...] = h_ref[...]


def _ssd(xbc3, z3, dtr3, conv_st8, ssm_st3, ssd_consts, layer, valid):
    conv_w, conv_b3, dtb3, alog3, dsk3, nw3, expand = ssd_consts
    b, l, _ = xbc3.shape
    lc = SSD_CHUNK
    vec = lambda wdt: pl.BlockSpec((None, 1, wdt), lambda bi, c: (layer, 0, 0))
    return pl.pallas_call(
        functools.partial(_ssd_kernel, valid=valid),
        grid=(b, l // lc),
        in_specs=[pl.BlockSpec((None, lc, CONV_DIM), lambda bi, c: (bi, c, 0)),
                  pl.BlockSpec((None, lc, SSD_WIDTH), lambda bi, c: (bi, c, 0)),
                  pl.BlockSpec((None, lc, LANES), lambda bi, c: (bi, c, 0)),
                  pl.BlockSpec((None, CONV_WIDTH, CONV_DIM), lambda bi, c: (layer, 0, 0)),
                  vec(CONV_DIM), vec(LANES), vec(LANES), vec(SSD_WIDTH), vec(SSD_WIDTH),
                  pl.BlockSpec((None, SUBLANES, CONV_DIM), lambda bi, c: (bi, 0, 0)),
                  pl.BlockSpec((None, SSD_HEADS * SSD_HEAD_DIM, SSD_STATE), lambda bi, c: (bi, 0, 0)),
                  pl.BlockSpec((LANES, SSD_WIDTH), lambda bi, c: (0, 0))],
        out_specs=[pl.BlockSpec((None, lc, SSD_WIDTH), lambda bi, c: (bi, c, 0)),
                   pl.BlockSpec((None, SSD_HEADS * SSD_HEAD_DIM, SSD_STATE), lambda bi, c: (bi, 0, 0))],
        out_shape=[jax.ShapeDtypeStruct((b, l, SSD_WIDTH), BF),
                   jax.ShapeDtypeStruct((b, SSD_HEADS * SSD_HEAD_DIM, SSD_STATE), F32)],
        scratch_shapes=[pltpu.VMEM((SSD_HEADS * SSD_HEAD_DIM, SSD_STATE), F32),
                        pltpu.VMEM((SUBLANES, CONV_DIM), F32),
                        pltpu.VMEM((SUBLANES + lc, CONV_DIM), F32)],
        compiler_params=_params("arbitrary", "arbitrary"),
        name="ssd",
    )(xbc3, z3, dtr3, conv_w, conv_b3, dtb3, alog3, dsk3, nw3, conv_st8, ssm_st3, expand)


def _outproj_kernel(a1_ref, a2_ref, w1_ref, w2_ref, x_ref, g_ref, o_ref, wb_ref):
    k1 = a1_ref.shape[1]

    @pl.when((pl.program_id(1) == 0) & (pl.program_id(2) == 0))
    def _():
        wb_ref[0:k1, :] = w1_ref[...].astype(BF)
        wb_ref[k1:, :] = w2_ref[...].astype(BF)

    acc = (jnp.dot(a1_ref[...], wb_ref[0:k1, :], preferred_element_type=F32)
           + jnp.dot(a2_ref[...], wb_ref[k1:, :], preferred_element_type=F32))
    o_ref[...] = x_ref[...] + g_ref[...] * acc


def _outproj(attn3, ssd3, w_out, layer, x3, mod, k_gate, tm, tn):
    b, l, k1 = attn3.shape
    k2 = ssd3.shape[2]
    per_tok = mod.shape[1] > 1
    nk = D_MODEL // tn
    mblk = (None, tm if per_tok else 1, tn)
    return pl.pallas_call(
        _outproj_kernel,
        grid=(nk, b, l // tm),
        in_specs=[pl.BlockSpec((None, tm, k1), lambda j, bi, i: (bi, i, 0)),
                  pl.BlockSpec((None, tm, k2), lambda j, bi, i: (bi, i, 0)),
                  pl.BlockSpec((None, k1, tn), lambda j, bi, i: (2 * layer, 0, j)),
                  pl.BlockSpec((None, k2, tn), lambda j, bi, i: (2 * layer + 1, 0, j)),
                  pl.BlockSpec((None, tm, tn), lambda j, bi, i: (bi, i, j)),
                  pl.BlockSpec(mblk, lambda j, bi, i: (bi, i if per_tok else 0, k_gate * nk + j))],
        out_specs=pl.BlockSpec((None, tm, tn), lambda j, bi, i: (bi, i, j)),
        out_shape=jax.ShapeDtypeStruct((b, l, D_MODEL), F32),
        scratch_shapes=[pltpu.VMEM((k1 + k2, tn), BF)],
        compiler_params=_params("arbitrary", "arbitrary", "arbitrary"),
        name="outproj",
    )(attn3, ssd3, w_out, w_out, x3, mod)


def _pack_bf16_pairs(h):
    n = h.shape[1] // 2
    bits = pltpu.bitcast(h.astype(BF).astype(F32), jnp.uint32)
    return bits[:, n:] | (bits[:, :n] >> 16)


def _unpack_bf16_pairs(w):
    lo = pltpu.bitcast(w << 16, F32).astype(BF)
    hi = pltpu.bitcast(w & jnp.uint32(0xFFFF0000), F32).astype(BF)
    return lo, hi


def _router_kernel(x_ref, w_ref, sc_ref, sh_ref, rw_ref, rb_ref, h_ref, idx_ref, gate_ref):
    x = x_ref[...]
    y = x * lax.rsqrt(jnp.mean(x * x, axis=-1, keepdims=True) + EPS) * w_ref[...]
    h = y * (1.0 + sc_ref[...]) + sh_ref[...]
    h_ref[...] = _pack_bf16_pairs(h)
    logits = jnp.dot(h, rw_ref[...], precision=HI, preferred_element_type=F32) + rb_ref[...]
    lane = lax.broadcasted_iota(I32, logits.shape, 1)
    lane_f = lane.astype(F32)
    work = jnp.where(lane < N_EXPERTS, logits, -jnp.inf)
    idx_out = jnp.zeros(logits.shape, F32)
    val_out = jnp.zeros(logits.shape, F32)
    top = None
    for k in range(TOP_K):
        m = jnp.max(work, axis=-1, keepdims=True)
        sel = jnp.min(jnp.where(work == m, lane_f, float(LANES)), axis=-1, keepdims=True)
        if k == 0:
            top = m
        idx_out = jnp.where(lane == k, sel, idx_out)
        val_out = jnp.where(lane == k, jnp.exp(m - top), val_out)
        work = jnp.where(lane_f == sel, -jnp.inf, work)
    idx_ref[...] = idx_out.astype(I32)
    gate_ref[...] = val_out / jnp.sum(val_out, axis=-1, keepdims=True)


def _router(x3, norm_w3, layer, mod, k_sh, k_sc, router_wp, router_bp, tm):
    b, l, _ = x3.shape
    per_tok = mod.shape[1] > 1
    mblk = (None, tm if per_tok else 1, D_MODEL)

    def mspec(k):
        return pl.BlockSpec(mblk, lambda bi, i: (bi, i if per_tok else 0, k))

    return pl.pallas_call(
        _router_kernel,
        grid=(b, l // tm),
        in_specs=[pl.BlockSpec((None, tm, D_MODEL), lambda bi, i: (bi, i, 0)),
                  pl.BlockSpec((None, 1, D_MODEL), lambda bi, i: (layer, 0, 0)),
                  mspec(k_sc), mspec(k_sh),
                  pl.BlockSpec((None, D_MODEL, LANES), lambda bi, i: (layer, 0, 0)),
                  pl.BlockSpec((None, 1, LANES), lambda bi, i: (layer, 0, 0))],
        out_specs=[pl.BlockSpec((None, tm, D_MODEL // 2), lambda bi, i: (bi, i, 0)),
                   pl.BlockSpec((None, tm, LANES), lambda bi, i: (bi, i, 0)),
                   pl.BlockSpec((None, tm, LANES), lambda bi, i: (bi, i, 0))],
        out_shape=[jax.ShapeDtypeStruct((b, l, D_MODEL // 2), jnp.uint32),
                   jax.ShapeDtypeStruct((b, l, LANES), I32),
                   jax.ShapeDtypeStruct((b, l, LANES), F32)],
        compiler_params=_params("arbitrary", "arbitrary"),
        name="router",
    )(x3, norm_w3, mod, mod, router_wp, router_bp)


def _row_copy(src_ref, src_row, dst_ref, dst_row, sem):
    return pltpu.make_async_copy(src_ref.at[pl.ds(src_row, 1)], dst_ref.at[pl.ds(dst_row, 1)], sem)


def _dispatch_kernel(slot_ref, h_ref, xs_in_ref, o_ref, sem, *, tt):
    del xs_in_ref
    i = pl.program_id(0)
    base = i * tt

    def issue(r, carry):
        for k in range(TOP_K):
            _row_copy(h_ref, base + r, o_ref, slot_ref[0, r * TOP_K + k], sem).start()
        return carry

    lax.fori_loop(0, tt, issue, 0)

    def drain(r, carry):
        for k in range(TOP_K):
            _row_copy(h_ref, 0, o_ref, 0, sem).wait()
        return carry

    @pl.when(i > 0)
    def _():
        lax.fori_loop(0, tt, drain, 0)

    @pl.when(i == pl.num_programs(0) - 1)
    def _():
        lax.fori_loop(0, tt, drain, 0)


def _dispatch(slots3, h_rows, xs):
    nsteps, _, w = slots3.shape
    tt = w // TOP_K
    return pl.pallas_call(
        functools.partial(_dispatch_kernel, tt=tt),
        grid=(nsteps,),
        in_specs=[pl.BlockSpec((None, 1, w), lambda i: (i, 0, 0), memory_space=pltpu.SMEM),
                  pl.BlockSpec(memory_space=pl.ANY),
                  pl.BlockSpec(memory_space=pl.ANY)],
        out_specs=pl.BlockSpec(memory_space=pl.ANY),
        out_shape=jax.ShapeDtypeStruct(xs.shape, xs.dtype),
        scratch_shapes=[pltpu.SemaphoreType.DMA(())],
        input_output_aliases={2: 0},
        compiler_params=_params("arbitrary"),
        name="moe_dispatch",
    )(slots3, h_rows, xs)


def _run_start(te_ref, i):
    return (i == 0) | (te_ref[i] != te_ref[jnp.maximum(i - 1, 0)])


def _weight_copy(w_ref, e, col, tn, buf_ref, sem_ref, slot, part):
    return pltpu.make_async_copy(w_ref.at[e, :, pl.ds(pl.multiple_of(col, LANES), tn)],
                                 buf_ref.at[slot, part], sem_ref.at[slot, part])


def _stream_weights(te_ref, nv_ref, nxt_ref, slot_ref, copies, cast):
    j = pl.program_id(0)
    i = pl.program_id(1)
    e = te_ref[i]

    @pl.when((i < nv_ref[0]) & _run_start(te_ref, i))
    def _():
        @pl.when((j == 0) & (i == 0))
        def _():
            slot_ref[0] = 0
            for cp in copies(e, j, 0):
                cp.start()

        slot = slot_ref[0]
        for cp in copies(e, j, slot):
            cp.wait()
        nxt = nxt_ref[e]
        jn = j + jnp.where(nxt <= e, 1, 0)

        @pl.when(jn < pl.num_programs(0))
        def _():
            for cp in copies(nxt, jn, 1 - slot):
                cp.start()

        cast(slot)
        slot_ref[0] = 1 - slot


def _moe_up_kernel(te_ref, nv_ref, nxt_ref, x_ref, w_ref, bg_ref, bu_ref, o_ref, wbuf_ref, wbf_ref, sem_ref,
                   slot_ref, *, e0):
    i = pl.program_id(1)
    valid = i < nv_ref[0]
    tn = o_ref.shape[1]
    half = x_ref.shape[1]

    def copies(e, j, slot):
        return [_weight_copy(w_ref, e0 + e, part * EXPERT_DIM + j * tn, tn, wbuf_ref, sem_ref, slot, part)
                for part in range(2)]

    def cast(slot):
        for part in range(2):
            wbf_ref[part] = wbuf_ref[slot, part].astype(BF)

    _stream_weights(te_ref, nv_ref, nxt_ref, slot_ref, copies, cast)

    @pl.when(valid)
    def _():
        lo, hi = _unpack_bf16_pairs(x_ref[...])

        def proj(part):
            return (jnp.dot(lo, wbf_ref[part, :half, :], preferred_element_type=F32)
                    + jnp.dot(hi, wbf_ref[part, half:, :], preferred_element_type=F32))

        g = proj(0) + bg_ref[...]
        u = proj(1) + bu_ref[...]
        g = jnp.minimum(g, SWIGLU_LIMIT)
        u = jnp.clip(u, -SWIGLU_LIMIT, SWIGLU_LIMIT)
        o_ref[...] = (g * jax.nn.sigmoid(SWIGLU_ALPHA * g) * (u + 1.0)).astype(o_ref.dtype)

    @pl.when(jnp.logical_not(valid))
    def _():
        o_ref[...] = jnp.zeros(o_ref.shape, o_ref.dtype)


def _moe_up(tile_e, n_valid, next_e, xs, w_gu, b_gu3, layer):
    ns, half = xs.shape
    t, tn = MOE_TILE, MOE_UP_TN
    nt = ns // t
    nj = EXPERT_DIM // tn
    e0 = layer * N_EXPERTS

    def row(i, nv):
        return jnp.minimum(i, nv[0] - 1)

    grid_spec = pltpu.PrefetchScalarGridSpec(
        num_scalar_prefetch=3,
        grid=(nj, nt),
        in_specs=[pl.BlockSpec((t, half), lambda j, i, te, nv, nx: (row(i, nv), 0)),
                  pl.BlockSpec(memory_space=pl.ANY),
                  pl.BlockSpec((None, 1, tn), lambda j, i, te, nv, nx: (e0 + te[i], 0, j)),
                  pl.BlockSpec((None, 1, tn), lambda j, i, te, nv, nx: (e0 + te[i], 0, nj + j))],
        out_specs=pl.BlockSpec((t, tn), lambda j, i, te, nv, nx: (i, j)),
        scratch_shapes=[pltpu.VMEM((2, 2, D_MODEL, tn), F32), pltpu.VMEM((2, D_MODEL, tn), BF),
                        pltpu.SemaphoreType.DMA((2, 2)), pltpu.SMEM((1,), I32)],
    )
    return pl.pallas_call(
        functools.partial(_moe_up_kernel, e0=e0),
        grid_spec=grid_spec,
        out_shape=jax.ShapeDtypeStruct((ns, EXPERT_DIM), BF),
        compiler_params=_params("arbitrary", "arbitrary"),
        name="moe_up",
    )(tile_e, n_valid, next_e, xs, w_gu, b_gu3, b_gu3)


def _moe_down_kernel(te_ref, nv_ref, nxt_ref, a_ref, w_ref, b_ref, o_ref, wbuf_ref, wbf_ref, sem_ref, slot_ref,
                     *, e0):
    i = pl.program_id(1)
    valid = i < nv_ref[0]
    tn = o_ref.shape[1]

    def copies(e, j, slot):
        return [_weight_copy(w_ref, e0 + e, j * tn, tn, wbuf_ref, sem_ref, slot, 0)]

    def cast(slot):
        wbf_ref[...] = wbuf_ref[slot, 0].astype(BF)

    _stream_weights(te_ref, nv_ref, nxt_ref, slot_ref, copies, cast)

    @pl.when(valid)
    def _():
        o_ref[...] = jnp.dot(a_ref[...], wbf_ref[...], preferred_element_type=F32) + b_ref[...]

    @pl.when(jnp.logical_not(valid))
    def _():
        o_ref[...] = jnp.zeros(o_ref.shape, o_ref.dtype)


def _moe_down(tile_e, n_valid, next_e, act, w_down, b_down3, layer):
    ns = act.shape[0]
    t, tn = MOE_TILE, MOE_DOWN_TN
    nt = ns // t
    nj = D_MODEL // tn
    e0 = layer * N_EXPERTS

    def row(i, nv):
        return jnp.minimum(i, nv[0] - 1)

    grid_spec = pltpu.PrefetchScalarGridSpec(
        num_scalar_prefetch=3,
        grid=(nj, nt),
        in_specs=[pl.BlockSpec((t, EXPERT_DIM), lambda j, i, te, nv, nx: (row(i, nv), 0)),
                  pl.BlockSpec(memory_space=pl.ANY),
                  pl.BlockSpec((None, 1, tn), lambda j, i, te, nv, nx: (e0 + te[i], 0, j))],
        out_specs=pl.BlockSpec((t, tn), lambda j, i, te, nv, nx: (i, j)),
        scratch_shapes=[pltpu.VMEM((2, 1, EXPERT_DIM, tn), F32), pltpu.VMEM((EXPERT_DIM, tn), BF),
                        pltpu.SemaphoreType.DMA((2, 1)), pltpu.SMEM((1,), I32)],
    )
    return pl.pallas_call(
        functools.partial(_moe_down_kernel, e0=e0),
        grid_spec=grid_spec,
        out_shape=jax.ShapeDtypeStruct((ns, D_MODEL), F32),
        compiler_params=_params("arbitrary", "arbitrary"),
        name="moe_down",
    )(tile_e, n_valid, next_e, act, w_down, b_down3)


def _combine_kernel(slot_ref, y_ref, gate_ref, x_ref, g_ref, o_ref, buf_ref, sem):
    tt = x_ref.shape[0]

    def issue(r, carry):
        for k in range(TOP_K):
            _row_copy(y_ref, slot_ref[0, r * TOP_K + k], buf_ref.at[k], r, sem).start()
        return carry

    lax.fori_loop(0, tt, issue, 0)

    def drain(r, carry):
        for k in range(TOP_K):
            _row_copy(y_ref, 0, buf_ref.at[k], r, sem).wait()
        return carry

    lax.fori_loop(0, tt, drain, 0)
    gates = gate_ref[...]
    acc = gates[:, 0:1] * buf_ref[0]
    for k in range(1, TOP_K):
        acc = acc + gates[:, k:k + 1] * buf_ref[k]
    o_ref[...] = x_ref[...] + g_ref[...] * acc


def _combine(slots3, y_slots, gates3, x3, mod, k_gate, tt):
    b, l, _ = x3.shape
    per_tok = mod.shape[1] > 1
    lt = l // tt
    mblk = (None, tt if per_tok else 1, D_MODEL)
    return pl.pallas_call(
        _combine_kernel,
        grid=(b, lt),
        in_specs=[pl.BlockSpec((None, 1, tt * TOP_K), lambda bi, i: (bi * lt + i, 0, 0), memory_space=pltpu.SMEM),
                  pl.BlockSpec(memory_space=pl.ANY),
                  pl.BlockSpec((None, tt, LANES), lambda bi, i: (bi, i, 0)),
                  pl.BlockSpec((None, tt, D_MODEL), lambda bi, i: (bi, i, 0)),
                  pl.BlockSpec(mblk, lambda bi, i: (bi, i if per_tok else 0, k_gate))],
        out_specs=pl.BlockSpec((None, tt, D_MODEL), lambda bi, i: (bi, i, 0)),
        out_shape=jax.ShapeDtypeStruct((b, l, D_MODEL), F32),
        scratch_shapes=[pltpu.VMEM((TOP_K, tt, D_MODEL), F32), pltpu.SemaphoreType.DMA(())],
        compiler_params=_params("arbitrary", "arbitrary"),
        name="moe_combine",
    )(slots3, y_slots, gates3, x3, mod)


def _routing_tables(top_idx, n_tokens):
    m = n_tokens * TOP_K
    t = MOE_TILE
    flat_e = top_idx.reshape(m)
    onehot = (flat_e[:, None] == jnp.arange(N_EXPERTS, dtype=I32)[None, :]).astype(I32)
    csum = jnp.cumsum(onehot, axis=0)
    pos = jnp.sum(csum * onehot, axis=1) - 1
    counts = csum[-1]
    tiles_e = (counts + t - 1) // t
    tile_end = jnp.cumsum(tiles_e)
    tile_start = tile_end - tiles_e
    slot = tile_start[flat_e] * t + pos
    n_tiles = m // t + N_EXPERTS
    n_valid = tile_end[-1]
    tile_ids = jnp.minimum(jnp.arange(n_tiles, dtype=I32), n_valid - 1)
    tile_e = jnp.minimum(jnp.sum((tile_end[None, :] <= tile_ids[:, None]).astype(I32), axis=1), N_EXPERTS - 1)
    ids = jnp.arange(N_EXPERTS, dtype=I32)
    nonempty = tiles_e > 0
    later = jnp.min(jnp.where(nonempty[None, :] & (ids[None, :] > ids[:, None]), ids[None, :], N_EXPERTS), axis=1)
    first = jnp.min(jnp.where(nonempty, ids, N_EXPERTS))
    next_e = jnp.where(later < N_EXPERTS, later, first)
    return slot.astype(I32), n_tiles * t, tile_e.astype(I32), n_valid.astype(I32).reshape(1), next_e.astype(I32)


def _mixing(xr, mod, b, l, pos, k_past, v_past, conv_state, ssm_state, weights, layer, tm):
    (norm_mix_w3, w_in, w_dt3, q_norm_w3, k_norm_w3, ssd_consts, w_out) = weights
    m = b * l
    h = _norm_mod(xr, norm_mix_w3, layer, mod, 0, 1, tm).reshape(m, D_MODEL)
    qkv = _matmul(h, w_in, layer, 0, QKV_WIDTH // 512, 512, tm)
    z = _matmul(h, w_in, layer, QKV_WIDTH // 512, SSD_WIDTH // 512, 512, tm)
    xbc = _matmul(h, w_in, layer, (QKV_WIDTH + SSD_WIDTH) // 512, CONV_DIM // 512, 512, tm)
    dtr = _matmul(h, w_dt3, layer, 0, 1, LANES, tm)

    cosf, sinf = _rope_tables(jnp.tile(pos, b))
    qn, kn = _qk_norm_rope(qkv, cosf, sinf, q_norm_w3, k_norm_w3, layer, tm)
    v = qkv[:, ATTN_WIDTH + KV_WIDTH:]

    if k_past is None:
        attn3 = _attn_prompt(qn.reshape(b, l, ATTN_WIDTH), kn.reshape(b, l, KV_WIDTH),
                             qkv.reshape(b, l, QKV_WIDTH))
        xbc3, z3, dtr3 = (a.reshape(b, l, -1) for a in (xbc, z, dtr))
        valid = SSD_CHUNK
    else:
        rows = Q_PER_KV * l
        qg = qn.reshape(b, l, KV_HEADS, Q_PER_KV, HEAD_DIM).transpose(0, 2, 3, 1, 4).reshape(b, KV_HEADS, rows, HEAD_DIM)
        padn = ((0, 0), (0, LANES - l), (0, 0))
        og = _attn_sample(qg, k_past, v_past, layer, jnp.pad(kn.reshape(b, l, KV_WIDTH), padn),
                          jnp.pad(v.reshape(b, l, KV_WIDTH), padn), l)
        attn3 = og.reshape(b, KV_HEADS, Q_PER_KV, l, HEAD_DIM).transpose(0, 3, 1, 2, 4).reshape(b, l, ATTN_WIDTH)
        attn3 = attn3.astype(BF)
        padc = ((0, 0), (0, SSD_CHUNK - l), (0, 0))
        xbc3, z3, dtr3 = (jnp.pad(a.reshape(b, l, -1), padc) for a in (xbc, z, dtr))
        valid = l

    conv_st8 = jnp.pad(conv_state, ((0, 0), (SUBLANES - (CONV_WIDTH - 1), 0), (0, 0)))
    ssd3, h_fin = _ssd(xbc3, z3, dtr3, conv_st8, ssm_state.reshape(b, SSD_HEADS * SSD_HEAD_DIM, SSD_STATE),
                       ssd_consts, layer, valid)
    ssd3 = ssd3[:, :l]
    x1 = _outproj(attn3.reshape(xr.shape[0], xr.shape[1], ATTN_WIDTH), ssd3.reshape(xr.shape[0], xr.shape[1], SSD_WIDTH),
                  w_out, layer, xr, mod, 2, tm, 512)
    keep = CONV_WIDTH - 1
    assert l >= keep
    new_conv = xbc.reshape(b, l, CONV_DIM)[:, l - keep:]
    return (x1, kn.reshape(b, l, KV_HEADS, HEAD_DIM), v.reshape(b, l, KV_HEADS, HEAD_DIM), new_conv,
            h_fin.reshape(b, SSD_HEADS, SSD_HEAD_DIM, SSD_STATE))


def kernel(x_prompt, x_sample, cache_k, cache_v, state_conv, state_ssm, c_prompt, c_sample, w_ada, b_ada, norm_mix_w, norm_ffn_w, w_in, q_norm_w, k_norm_w, conv_w, conv_b, dt_bias, a_log, d_skip, ssd_norm_w, w_out, router_w, router_b, w_gate_up, b_gate_up, w_down, b_down):
    depth = w_ada.shape[0]
    bp, seq, _ = x_prompt.shape
    bs, dseq, _ = x_sample.shape
    kv_win = cache_k.shape[2]
    assert kv_win == MAX_REACH and PAST_LEN >= MAX_REACH
    n_p, n_s = bp * seq, bs * dseq
    n_tok = n_p + n_s
    keep_p = min(MAX_REACH, seq)

    vec3 = lambda a: a.reshape(depth, 1, -1)
    pad_lanes = lambda a: jnp.pad(a, [(0, 0)] * (a.ndim - 1) + [(0, LANES - a.shape[-1])])
    b_ada3 = vec3(b_ada)
    norm_mix_w3, norm_ffn_w3 = vec3(norm_mix_w), vec3(norm_ffn_w)
    w_dt3 = pad_lanes(w_in[:, :, DT_COL0:])
    expand = (jnp.arange(LANES, dtype=I32)[:, None]
              == jnp.arange(SSD_WIDTH, dtype=I32)[None, :] // SSD_HEAD_DIM).astype(F32)
    ssd_consts = (conv_w, vec3(conv_b), vec3(pad_lanes(dt_bias)), vec3(pad_lanes(a_log)),
                  vec3(jnp.repeat(d_skip, SSD_HEAD_DIM, axis=-1)), vec3(ssd_norm_w), expand)
    weights = (norm_mix_w3, w_in, w_dt3, vec3(q_norm_w), vec3(k_norm_w), ssd_consts,
               w_out.reshape(depth, 2, MIX_WIDTH // 2, D_MODEL).reshape(depth * 2, MIX_WIDTH // 2, D_MODEL))
    router_wp = pad_lanes(router_w)
    router_bp = vec3(pad_lanes(router_b))
    w_gu = w_gate_up.reshape(depth * N_EXPERTS, D_MODEL, 2 * EXPERT_DIM)
    b_gu3 = b_gate_up.reshape(depth * N_EXPERTS, 1, 2 * EXPERT_DIM)
    w_dn = w_down.reshape(depth * N_EXPERTS, EXPERT_DIM, D_MODEL)
    b_dn3 = b_down.reshape(depth * N_EXPERTS, 1, D_MODEL)

    n_c = bp + bs
    c_all = jnp.concatenate([c_prompt, c_sample, jnp.zeros((-n_c % SUBLANES, D_MODEL), F32)], axis=0)
    pos_p = jnp.arange(seq, dtype=I32)
    pos_s = PAST_LEN + jnp.arange(dseq, dtype=I32)

    y_p = x_prompt
    y_s = x_sample.reshape(1, n_s, D_MODEL)
    outs = [[] for _ in range(8)]
    for layer in range(depth):
        mod = _ada(c_all, w_ada, b_ada3, layer)
        mod_p = mod[:bp].reshape(bp, 1, 6 * D_MODEL)
        mod_s = jnp.repeat(mod[bp:n_c], dseq, axis=0).reshape(1, n_s, 6 * D_MODEL)

        zero_conv = jnp.zeros((bp, CONV_WIDTH - 1, CONV_DIM), F32)
        zero_ssm = jnp.zeros((bp, SSD_HEADS, SSD_HEAD_DIM, SSD_STATE), F32)
        x1_p, kp, vp, cp, sp = _mixing(y_p, mod_p, bp, seq, pos_p, None, None, zero_conv, zero_ssm,
                                       weights, layer, 512)
        x1_s, kn, vn, cn, sn = _mixing(y_s, mod_s, bs, dseq, pos_s, cache_k, cache_v, state_conv[layer],
                                       state_ssm[layer], weights, layer, n_s)

        h2_p, idx_p, gate_p = _router(x1_p, norm_ffn_w3, layer, mod_p, 3, 4, router_wp, router_bp, 256)
        h2_s, idx_s, gate_s = _router(x1_s, norm_ffn_w3, layer, mod_s, 3, 4, router_wp, router_bp, n_s)
        top_idx = jnp.concatenate([idx_p.reshape(n_p, LANES)[:, :TOP_K], idx_s.reshape(n_s, LANES)[:, :TOP_K]], axis=0)
        slot, n_slots, tile_e, n_valid, next_e = _routing_tables(top_idx, n_tok)
        slot_p, slot_s = slot[:n_p * TOP_K], slot[n_p * TOP_K:]
        td = 256
        xs = jnp.zeros((n_slots, D_MODEL // 2), jnp.uint32)
        xs = _dispatch(slot_p.reshape(n_p // td, 1, td * TOP_K), h2_p.reshape(n_p, D_MODEL // 2), xs)
        xs = _dispatch(slot_s.reshape(1, 1, n_s * TOP_K), h2_s.reshape(n_s, D_MODEL // 2), xs)
        act = _moe_up(tile_e, n_valid, next_e, xs, w_gu, b_gu3, layer)
        y_slots = _moe_down(tile_e, n_valid, next_e, act, w_dn, b_dn3, layer)
        tt = 128
        y_p = _combine(slot_p.reshape(n_p // tt, 1, tt * TOP_K), y_slots, gate_p, x1_p, mod_p, 5, tt)
        y_s = _combine(slot_s.reshape(n_s // tt, 1, tt * TOP_K), y_slots, gate_s, x1_s, mod_s, 5, tt)

        for lst, val in zip(outs, (kp[:, seq - keep_p:], vp[:, seq - keep_p:], cp, sp, kn, vn, cn, sn)):
            lst.append(val)
    return (y_p, y_s.reshape(bs, dseq, D_MODEL)) + tuple(jnp.stack(o) for o in outs)
```

```python
import functools
import math

import jax
import jax.numpy as jnp
from jax import lax
from jax.experimental import pallas as pl
from jax.experimental.pallas import tpu as pltpu

F32 = jnp.float32
BF = jnp.bfloat16
I32 = jnp.int32

D_MODEL = 4096
PAST_LEN = 8192
HEAD_DIM = 128
ATTN_HEADS = 16
KV_HEADS = 4
Q_PER_KV = ATTN_HEADS // KV_HEADS
ATTN_WIDTH = ATTN_HEADS * HEAD_DIM
KV_WIDTH = KV_HEADS * HEAD_DIM
ROT_DIM = HEAD_DIM // 4
ROPE_THETA = 500000.0
DILATED_BRANCHES = ((128, 1), (512, 4), (2048, 16))
MAX_REACH = 2048
SSD_WIDTH = 2048
SSD_HEAD_DIM = 64
SSD_HEADS = 32
SSD_GROUPS = 8
SSD_STATE = 128
CONV_WIDTH = 4
CONV_DIM = SSD_WIDTH + 2 * SSD_GROUPS * SSD_STATE
SSD_CHUNK = 128
MIX_WIDTH = ATTN_WIDTH + SSD_WIDTH
N_EXPERTS = 32
TOP_K = 4
EXPERT_DIM = D_MODEL
SWIGLU_LIMIT = 7.0
SWIGLU_ALPHA = 1.702
EPS = 1e-6

LANES = 128
SUBLANES = 8
NEG = -1e30
QKV_WIDTH = ATTN_WIDTH + 2 * KV_WIDTH
DT_COL0 = QKV_WIDTH + SSD_WIDTH + CONV_DIM
MOE_TILE = 256
MOE_UP_TN = 512
MOE_DOWN_TN = 1024
VMEM_LIMIT = 56 * 2 ** 20
WEIGHT_DMA_PRIORITY = 1
HI = lax.Precision.HIGHEST
NT_DIMS = (((1,), (1,)), ((), ()))
TN_DIMS = (((0,), (0,)), ((), ()))


def _params(*sem):
    return pltpu.CompilerParams(dimension_semantics=sem, vmem_limit_bytes=VMEM_LIMIT)


def _silu(x):
    return x * jax.nn.sigmoid(x)


def _ada_kernel(c_ref, w_ref, b_ref, o_ref):
    a = _silu(c_ref[...]).astype(BF)
    o_ref[...] = jnp.dot(a, w_ref[...].astype(BF), preferred_element_type=F32) + b_ref[...]


def _ada(c, w_ada, b_ada3, layer):
    mc = c.shape[0]
    n = 6 * D_MODEL
    tn = 512
    return pl.pallas_call(
        _ada_kernel,
        grid=(n // tn,),
        in_specs=[pl.BlockSpec((mc, D_MODEL), lambda j: (0, 0)),
                  pl.BlockSpec((None, D_MODEL, tn), lambda j: (layer, 0, j)),
                  pl.BlockSpec((None, 1, tn), lambda j: (layer, 0, j))],
        out_specs=pl.BlockSpec((mc, tn), lambda j: (0, j)),
        out_shape=jax.ShapeDtypeStruct((mc, n), F32),
        compiler_params=_params("arbitrary"),
        name="ada",
    )(c, w_ada, b_ada3)


def _norm_mod_kernel(x_ref, w_ref, sc_ref, sh_ref, o_ref):
    x = x_ref[...]
    y = x * lax.rsqrt(jnp.mean(x * x, axis=-1, keepdims=True) + EPS) * w_ref[...]
    o_ref[...] = (y * (1.0 + sc_ref[...]) + sh_ref[...]).astype(o_ref.dtype)


def _norm_mod(x3, norm_w3, layer, mod, k_sh, k_sc, tm):
    b, l, _ = x3.shape
    per_tok = mod.shape[1] > 1
    mblk = (None, tm if per_tok else 1, D_MODEL)

    def mspec(k):
        return pl.BlockSpec(mblk, lambda bi, i: (bi, i if per_tok else 0, k))

    return pl.pallas_call(
        _norm_mod_kernel,
        grid=(b, l // tm),
        in_specs=[pl.BlockSpec((None, tm, D_MODEL), lambda bi, i: (bi, i, 0)),
                  pl.BlockSpec((None, 1, D_MODEL), lambda bi, i: (layer, 0, 0)),
                  mspec(k_sc), mspec(k_sh)],
        out_specs=pl.BlockSpec((None, tm, D_MODEL), lambda bi, i: (bi, i, 0)),
        out_shape=jax.ShapeDtypeStruct((b, l, D_MODEL), BF),
        compiler_params=_params("arbitrary", "arbitrary"),
        name="norm_mod",
    )(x3, norm_w3, mod, mod)


def _mm_kernel(a_ref, w_ref, o_ref, wb_ref):
    @pl.when(pl.program_id(1) == 0)
    def _():
        wb_ref[...] = w_ref[...].astype(BF)

    o_ref[...] = jnp.dot(a_ref[...], wb_ref[...], preferred_element_type=F32)


def _matmul(a, w3, layer, col_blk0, n_tiles, tn, tm):
    m, k = a.shape
    return pl.pallas_call(
        _mm_kernel,
        grid=(n_tiles, m // tm),
        in_specs=[pl.BlockSpec((tm, k), lambda j, i: (i, 0)),
                  pl.BlockSpec((None, k, tn), lambda j, i: (layer, 0, col_blk0 + j))],
        out_specs=pl.BlockSpec((tm, tn), lambda j, i: (i, j)),
        out_shape=jax.ShapeDtypeStruct((m, n_tiles * tn), F32),
        scratch_shapes=[pltpu.VMEM((k, tn), BF)],
        compiler_params=_params("arbitrary", "arbitrary"),
        name="matmul",
    )(a, w3)


def _rope_tables(pos):
    half = ROT_DIM // 2
    inv_freq = jnp.power(ROPE_THETA, -jnp.arange(half, dtype=F32) * 2.0 / ROT_DIM)
    ang = pos.astype(F32)[:, None] * inv_freq[None, :]
    cos, sin = jnp.cos(ang), jnp.sin(ang)
    n = pos.shape[0]
    cosf = jnp.concatenate([cos, cos, jnp.ones((n, HEAD_DIM - ROT_DIM), F32)], axis=1)
    sinf = jnp.concatenate([-sin, sin, jnp.zeros((n, HEAD_DIM - ROT_DIM), F32)], axis=1)
    return cosf, sinf


def _norm_rope_head(xh, w, cosf, sinf, lane):
    half = ROT_DIM // 2
    y = xh * lax.rsqrt(jnp.mean(xh * xh, axis=-1, keepdims=True) + EPS) * w
    partner = jnp.where(lane < half, pltpu.roll(y, HEAD_DIM - half, 1), pltpu.roll(y, half, 1))
    return y * cosf + partner * sinf


def _qk_kernel(q_ref, k_ref, cos_ref, sin_ref, qw_ref, kw_ref, qo_ref, ko_ref):
    cosf = cos_ref[...]
    sinf = sin_ref[...]
    lane = lax.broadcasted_iota(I32, cosf.shape, 1)
    scale = HEAD_DIM ** -0.5
    for h in range(ATTN_HEADS):
        sl = slice(h * HEAD_DIM, (h + 1) * HEAD_DIM)
        qo_ref[:, sl] = _norm_rope_head(q_ref[:, sl], qw_ref[...], cosf, sinf, lane) * scale
    for h in range(KV_HEADS):
        sl = slice(h * HEAD_DIM, (h + 1) * HEAD_DIM)
        ko_ref[:, sl] = _norm_rope_head(k_ref[:, sl], kw_ref[...], cosf, sinf, lane)


def _qk_norm_rope(qkv, cosf, sinf, q_norm_w3, k_norm_w3, layer, tm):
    m = qkv.shape[0]
    return pl.pallas_call(
        _qk_kernel,
        grid=(m // tm,),
        in_specs=[pl.BlockSpec((tm, ATTN_WIDTH), lambda i: (i, 0)),
                  pl.BlockSpec((tm, KV_WIDTH), lambda i: (i, ATTN_WIDTH // KV_WIDTH)),
                  pl.BlockSpec((tm, HEAD_DIM), lambda i: (i, 0)),
                  pl.BlockSpec((tm, HEAD_DIM), lambda i: (i, 0)),
                  pl.BlockSpec((None, 1, HEAD_DIM), lambda i: (layer, 0, 0)),
                  pl.BlockSpec((None, 1, HEAD_DIM), lambda i: (layer, 0, 0))],
        out_specs=[pl.BlockSpec((tm, ATTN_WIDTH), lambda i: (i, 0)),
                   pl.BlockSpec((tm, KV_WIDTH), lambda i: (i, 0))],
        out_shape=[jax.ShapeDtypeStruct((m, ATTN_WIDTH), F32),
                   jax.ShapeDtypeStruct((m, KV_WIDTH), F32)],
        compiler_params=_params("arbitrary"),
        name="qk_norm_rope",
    )(qkv, qkv, cosf, sinf, q_norm_w3, k_norm_w3)


def _multiplicity(delta):
    ge0 = delta >= 0
    cnt = jnp.zeros(delta.shape, F32)
    for window, dil in DILATED_BRANCHES:
        hit = ge0 & (delta <= window) & ((delta & (dil - 1)) == 0)
        cnt = cnt + jnp.where(hit, 1.0, 0.0)
    return cnt


ATTN_QB = 256
ATTN_KB = 128


def _attn_prompt_kernel(q_ref, k_ref, v_ref, tab_ref, o_ref):
    c = pl.program_id(2)
    qb = q_ref.shape[0]
    kb = tab_ref.shape[1]
    per = qb // kb
    q = q_ref[...]
    qs = jnp.concatenate([q[:, h * HEAD_DIM:(h + 1) * HEAD_DIM] for h in range(Q_PER_KV)], axis=0).astype(BF)
    last_kb = per * c + per - 1
    n_it = jnp.minimum(last_kb + 1, tab_ref.shape[0])

    def body(t, carry):
        m, l, acc = carry
        start = pl.multiple_of((last_kb - t) * kb, kb)
        kblk = k_ref[pl.ds(start, kb), :].astype(BF)
        vblk = v_ref[pl.ds(start, kb), :].astype(BF)
        cnt = jnp.concatenate([tab_ref[t]] * Q_PER_KV, axis=1)
        s = lax.dot_general(kblk, qs, NT_DIMS, preferred_element_type=F32)
        sm = jnp.where(cnt > 0.0, s, NEG)
        m_new = jnp.maximum(m, jnp.max(sm, axis=0, keepdims=True))
        p = cnt * jnp.exp(sm - m_new)
        alpha = jnp.exp(m - m_new)
        l = alpha * l + jnp.sum(p, axis=0, keepdims=True)
        acc = alpha * acc + lax.dot_general(vblk, p.astype(BF), TN_DIMS, preferred_element_type=F32)
        return m_new, l, acc

    cols = Q_PER_KV * qb
    init = (jnp.full((1, cols), NEG, F32), jnp.zeros((1, cols), F32), jnp.zeros((HEAD_DIM, cols), F32))
    _, l, acc = lax.fori_loop(0, n_it, body, init)
    out_t = acc / l
    for h in range(Q_PER_KV):
        o_ref[:, h * HEAD_DIM:(h + 1) * HEAD_DIM] = out_t[:, h * qb:(h + 1) * qb].T.astype(o_ref.dtype)


def _attn_prompt(qn3, kn3, qkv3):
    b, l, _ = qn3.shape
    qb, kb = ATTN_QB, ATTN_KB
    assert l % qb == 0
    gw = Q_PER_KV * HEAD_DIM
    v_blk0 = (ATTN_WIDTH + KV_WIDTH) // HEAD_DIM
    n_tab = MAX_REACH // kb + qb // kb
    tt = jnp.arange(n_tab, dtype=I32)[:, None, None]
    jj = jnp.arange(kb, dtype=I32)[None, :, None]
    ii = jnp.arange(qb, dtype=I32)[None, None, :]
    tables = _multiplicity((tt - (qb // kb - 1)) * kb + ii - jj)
    return pl.pallas_call(
        _attn_prompt_kernel,
        grid=(b, KV_HEADS, l // qb),
        in_specs=[pl.BlockSpec((None, qb, gw), lambda bi, g, c: (bi, c, g)),
                  pl.BlockSpec((None, l, HEAD_DIM), lambda bi, g, c: (bi, 0, g)),
                  pl.BlockSpec((None, l, HEAD_DIM), lambda bi, g, c: (bi, 0, v_blk0 + g)),
                  pl.BlockSpec((n_tab, kb, qb), lambda bi, g, c: (0, 0, 0))],
        out_specs=pl.BlockSpec((None, qb, gw), lambda bi, g, c: (bi, c, g)),
        out_shape=jax.ShapeDtypeStruct((b, l, ATTN_WIDTH), BF),
        compiler_params=_params("arbitrary", "arbitrary", "arbitrary"),
        name="attn_prompt",
    )(qn3, kn3, qkv3, tables)


def _attn_sample_kernel(q_ref, ck_ref, cv_ref, kn_ref, vn_ref, o_ref, *, n_new):
    w = ck_ref.shape[0] // KV_HEADS
    pad = kn_ref.shape[0]
    rows = q_ref.shape[1]
    ri = lax.broadcasted_iota(I32, (rows, w + pad), 0) & (n_new - 1)
    cj = lax.broadcasted_iota(I32, (rows, w + pad), 1)
    cnt = _multiplicity(w + ri - cj)
    for g in range(KV_HEADS):
        sl = slice(g * HEAD_DIM, (g + 1) * HEAD_DIM)
        qg = q_ref[g].astype(BF)
        s_c = lax.dot_general(qg, ck_ref[pl.ds(g, w, stride=KV_HEADS), :].astype(BF), NT_DIMS,
                              preferred_element_type=F32)
        s_n = lax.dot_general(qg, kn_ref[:, sl].astype(BF), NT_DIMS, preferred_element_type=F32)
        s = jnp.concatenate([s_c, s_n], axis=1)
        sm = jnp.where(cnt > 0.0, s, NEG)
        m = jnp.max(sm, axis=-1, keepdims=True)
        p = cnt * jnp.exp(sm - m)
        l = jnp.sum(p, axis=-1, keepdims=True)
        pb = p.astype(BF)
        o = (jnp.dot(pb[:, :w], cv_ref[pl.ds(g, w, stride=KV_HEADS), :].astype(BF), preferred_element_type=F32)
             + jnp.dot(pb[:, w:], vn_ref[:, sl].astype(BF), preferred_element_type=F32))
        o_ref[g] = o / l


def _attn_sample(qg4, cache_k, cache_v, layer, knp3, vnp3, n_new):
    b, _, rows, _ = qg4.shape
    depth, _, w = cache_k.shape[:3]
    pad = knp3.shape[1]
    cache_k, cache_v = (a.reshape(depth, b, w * KV_HEADS, HEAD_DIM) for a in (cache_k, cache_v))
    cache_spec = pl.BlockSpec((None, None, w * KV_HEADS, HEAD_DIM), lambda i: (layer, i, 0, 0))
    return pl.pallas_call(
        functools.partial(_attn_sample_kernel, n_new=n_new),
        grid=(b,),
        in_specs=[pl.BlockSpec((None, KV_HEADS, rows, HEAD_DIM), lambda i: (i, 0, 0, 0)),
                  cache_spec, cache_spec,
                  pl.BlockSpec((None, pad, KV_WIDTH), lambda i: (i, 0, 0)),
                  pl.BlockSpec((None, pad, KV_WIDTH), lambda i: (i, 0, 0))],
        out_specs=pl.BlockSpec((None, KV_HEADS, rows, HEAD_DIM), lambda i: (i, 0, 0, 0)),
        out_shape=jax.ShapeDtypeStruct((b, KV_HEADS, rows, HEAD_DIM), F32),
        compiler_params=_params("arbitrary"),
        name="attn_sample",
    )(qg4, cache_k, cache_v, knp3, vnp3)


def _ssd_kernel(x_ref, z_ref, dtr_ref, cw_ref, cb_ref, dtb_ref, alog_ref, dsk_ref, nw_ref, cst_ref, sst_ref,
                e_ref, y_ref, hout_ref, h_ref, tail_ref, buf_ref, *, valid):
    lc = x_ref.shape[0]
    c = pl.program_id(1)
    gw = SSD_WIDTH // SSD_GROUPS
    hpg = SSD_HEADS // SSD_GROUPS
    b_col0 = SSD_WIDTH
    c_col0 = SSD_WIDTH + SSD_GROUPS * SSD_STATE

    @pl.when(c == 0)
    def _():
        h_ref[...] = sst_ref[...]
        tail_ref[...] = cst_ref[...]

    x = x_ref[...]
    buf_ref[0:SUBLANES, :] = tail_ref[...]
    buf_ref[SUBLANES:SUBLANES + lc, :] = x
    tail_ref[...] = x[lc - SUBLANES:lc, :]
    conv = cb_ref[...]
    for j in range(CONV_WIDTH):
        r0 = SUBLANES - (CONV_WIDTH - 1) + j
        conv = conv + cw_ref[j:j + 1, :] * buf_ref[r0:r0 + lc, :]
    u = _silu(conv)
    xs = u[:, :SSD_WIDTH]

    dtv = dtr_ref[...] + dtb_ref[...]
    dt = jnp.maximum(dtv, 0.0) + jnp.log1p(jnp.exp(-jnp.abs(dtv)))
    row = lax.broadcasted_iota(I32, (lc, lc), 0)
    col = lax.broadcasted_iota(I32, (lc, lc), 1)
    if valid < lc:
        dt = jnp.where(lax.broadcasted_iota(I32, dt.shape, 0) < valid, dt, 0.0)
    da = dt * (-jnp.exp(alog_ref[...]))
    causal = row >= col
    acs = jnp.dot(jnp.where(causal, 1.0, 0.0), da, precision=HI, preferred_element_type=F32)
    acs_t = acs.T
    last = acs[lc - 1:lc, :]
    expand = e_ref[...]
    dt_e = jnp.dot(dt, expand, precision=HI, preferred_element_type=F32)
    dfs_e = jnp.dot(jnp.exp(acs), expand, precision=HI, preferred_element_type=F32)
    dte_e = jnp.dot(jnp.exp(last - acs), expand, precision=HI, preferred_element_type=F32)
    xdt = xs * dt_e
    xdtw = xdt * dte_e
    lane = lax.broadcasted_iota(I32, (lc, LANES), 1)
    z = z_ref[...]

    for g in range(SSD_GROUPS):
        gs = slice(g * gw, (g + 1) * gw)
        bg = u[:, b_col0 + g * SSD_STATE:b_col0 + (g + 1) * SSD_STATE].astype(BF)
        cg = u[:, c_col0 + g * SSD_STATE:c_col0 + (g + 1) * SSD_STATE].astype(BF)
        cb = lax.dot_general(cg, bg, NT_DIMS, preferred_element_type=F32)
        hg = h_ref[g * gw:(g + 1) * gw, :]
        y_off = lax.dot_general(cg, hg.astype(BF), NT_DIMS, preferred_element_type=F32)
        pieces = []
        for pr in range(hpg // 2):
            xp = xdt[:, g * gw + pr * LANES:g * gw + (pr + 1) * LANES]
            acc = jnp.zeros((lc, LANES), F32)
            for hh in range(2):
                h = g * hpg + pr * 2 + hh
                diff = acs[:, h:h + 1] - acs_t[h:h + 1, :]
                decay = jnp.exp(jnp.where(causal, diff, -jnp.inf))
                mat = (cb * decay).astype(BF)
                keep = (lane < SSD_HEAD_DIM) if hh == 0 else (lane >= SSD_HEAD_DIM)
                acc = acc + jnp.dot(mat, jnp.where(keep, xp, 0.0).astype(BF), preferred_element_type=F32)
            pieces.append(acc)
        y_diag = jnp.concatenate(pieces, axis=1)
        yg = y_diag + y_off * dfs_e[:, gs] + xs[:, gs] * dsk_ref[:, gs]

        ds = lax.dot_general(xdtw[:, gs].astype(BF), bg, TN_DIMS, preferred_element_type=F32)
        for hh in range(hpg):
            h = g * hpg + hh
            hs = slice(hh * SSD_HEAD_DIM, (hh + 1) * SSD_HEAD_DIM)
            h_ref[g * gw + hh * SSD_HEAD_DIM:g * gw + (hh + 1) * SSD_HEAD_DIM, :] = (
                hg[hs] * jnp.exp(last[:, h:h + 1]) + ds[hs])

        yg = yg * _silu(z[:, gs])
        yg = yg * lax.rsqrt(jnp.mean(yg * yg, axis=-1, keepdims=True) + EPS)
        y_ref[:, gs] = (yg * nw_ref[:, gs]).astype(y_ref.dtype)

    @pl.when(c == pl.num_programs(1) - 1)
    def _():
        hout_ref[...] = h_ref[...]


def _ssd(xbc3, z3, dtr3, conv_st8, ssm_st3, ssd_consts, layer, valid):
    conv_w, conv_b3, dtb3, alog3, dsk3, nw3, expand = ssd_consts
    b, l, _ = xbc3.shape
    lc = SSD_CHUNK
    vec = lambda wdt: pl.BlockSpec((None, 1, wdt), lambda bi, c: (layer, 0, 0))
    return pl.pallas_call(
        functools.partial(_ssd_kernel, valid=valid),
        grid=(b, l // lc),
        in_specs=[pl.BlockSpec((None, lc, CONV_DIM), lambda bi, c: (bi, c, 0)),
                  pl.BlockSpec((None, lc, SSD_WIDTH), lambda bi, c: (bi, c, 0)),
                  pl.BlockSpec((None, lc, LANES), lambda bi, c: (bi, c, 0)),
                  pl.BlockSpec((None, CONV_WIDTH, CONV_DIM), lambda bi, c: (layer, 0, 0)),
                  vec(CONV_DIM), vec(LANES), vec(LANES), vec(SSD_WIDTH), vec(SSD_WIDTH),
                  pl.BlockSpec((None, SUBLANES, CONV_DIM), lambda bi, c: (bi, 0, 0)),
                  pl.BlockSpec((None, SSD_HEADS * SSD_HEAD_DIM, SSD_STATE), lambda bi, c: (bi, 0, 0)),
                  pl.BlockSpec((LANES, SSD_WIDTH), lambda bi, c: (0, 0))],
        out_specs=[pl.BlockSpec((None, lc, SSD_WIDTH), lambda bi, c: (bi, c, 0)),
                   pl.BlockSpec((None, SSD_HEADS * SSD_HEAD_DIM, SSD_STATE), lambda bi, c: (bi, 0, 0))],
        out_shape=[jax.ShapeDtypeStruct((b, l, SSD_WIDTH), BF),
                   jax.ShapeDtypeStruct((b, SSD_HEADS * SSD_HEAD_DIM, SSD_STATE), F32)],
        scratch_shapes=[pltpu.VMEM((SSD_HEADS * SSD_HEAD_DIM, SSD_STATE), F32),
                        pltpu.VMEM((SUBLANES, CONV_DIM), F32),
                        pltpu.VMEM((SUBLANES + lc, CONV_DIM), F32)],
        compiler_params=_params("arbitrary", "arbitrary"),
        name="ssd",
    )(xbc3, z3, dtr3, conv_w, conv_b3, dtb3, alog3, dsk3, nw3, conv_st8, ssm_st3, expand)


def _outproj_kernel(a1_ref, a2_ref, w1_ref, w2_ref, x_ref, g_ref, o_ref, wb_ref):
    k1 = a1_ref.shape[1]

    @pl.when((pl.program_id(1) == 0) & (pl.program_id(2) == 0))
    def _():
        wb_ref[0:k1, :] = w1_ref[...].astype(BF)
        wb_ref[k1:, :] = w2_ref[...].astype(BF)

    acc = (jnp.dot(a1_ref[...], wb_ref[0:k1, :], preferred_element_type=F32)
           + jnp.dot(a2_ref[...], wb_ref[k1:, :], preferred_element_type=F32))
    o_ref[...] = x_ref[...] + g_ref[...] * acc


def _outproj(attn3, ssd3, w_out, layer, x3, mod, k_gate, tm, tn):
    b, l, k1 = attn3.shape
    k2 = ssd3.shape[2]
    per_tok = mod.shape[1] > 1
    nk = D_MODEL // tn
    mblk = (None, tm if per_tok else 1, tn)
    return pl.pallas_call(
        _outproj_kernel,
        grid=(nk, b, l // tm),
        in_specs=[pl.BlockSpec((None, tm, k1), lambda j, bi, i: (bi, i, 0)),
                  pl.BlockSpec((None, tm, k2), lambda j, bi, i: (bi, i, 0)),
                  pl.BlockSpec((None, k1, tn), lambda j, bi, i: (2 * layer, 0, j)),
                  pl.BlockSpec((None, k2, tn), lambda j, bi, i: (2 * layer + 1, 0, j)),
                  pl.BlockSpec((None, tm, tn), lambda j, bi, i: (bi, i, j)),
                  pl.BlockSpec(mblk, lambda j, bi, i: (bi, i if per_tok else 0, k_gate * nk + j))],
        out_specs=pl.BlockSpec((None, tm, tn), lambda j, bi, i: (bi, i, j)),
        out_shape=jax.ShapeDtypeStruct((b, l, D_MODEL), F32),
        scratch_shapes=[pltpu.VMEM((k1 + k2, tn), BF)],
        compiler_params=_params("arbitrary", "arbitrary", "arbitrary"),
        name="outproj",
    )(attn3, ssd3, w_out, w_out, x3, mod)


def _pack_bf16_pairs(h):
    n = h.shape[1] // 2
    bits = pltpu.bitcast(h.astype(BF).astype(F32), jnp.uint32)
    return bits[:, n:] | (bits[:, :n] >> 16)


def _unpack_bf16_pairs(w):
    lo = pltpu.bitcast(w << 16, F32).astype(BF)
    hi = pltpu.bitcast(w & jnp.uint32(0xFFFF0000), F32).astype(BF)
    return lo, hi


def _router_kernel(x_ref, w_ref, sc_ref, sh_ref, rw_ref, rb_ref, h_ref, idx_ref, gate_ref):
    x = x_ref[...]
    y = x * lax.rsqrt(jnp.mean(x * x, axis=-1, keepdims=True) + EPS) * w_ref[...]
    h = y * (1.0 + sc_ref[...]) + sh_ref[...]
    h_ref[...] = _pack_bf16_pairs(h)
    logits = jnp.dot(h, rw_ref[...], precision=HI, preferred_element_type=F32) + rb_ref[...]
    lane = lax.broadcasted_iota(I32, logits.shape, 1)
    lane_f = lane.astype(F32)
    work = jnp.where(lane < N_EXPERTS, logits, -jnp.inf)
    idx_out = jnp.zeros(logits.shape, F32)
    val_out = jnp.zeros(logits.shape, F32)
    top = None
    for k in range(TOP_K):
        m = jnp.max(work, axis=-1, keepdims=True)
        sel = jnp.min(jnp.where(work == m, lane_f, float(LANES)), axis=-1, keepdims=True)
        if k == 0:
            top = m
        idx_out = jnp.where(lane == k, sel, idx_out)
        val_out = jnp.where(lane == k, jnp.exp(m - top), val_out)
        work = jnp.where(lane_f == sel, -jnp.inf, work)
    idx_ref[...] = idx_out.astype(I32)
    gate_ref[...] = val_out / jnp.sum(val_out, axis=-1, keepdims=True)


def _router(x3, norm_w3, layer, mod, k_sh, k_sc, router_wp, router_bp, tm):
    b, l, _ = x3.shape
    per_tok = mod.shape[1] > 1
    mblk = (None, tm if per_tok else 1, D_MODEL)

    def mspec(k):
        return pl.BlockSpec(mblk, lambda bi, i: (bi, i if per_tok else 0, k))

    return pl.pallas_call(
        _router_kernel,
        grid=(b, l // tm),
        in_specs=[pl.BlockSpec((None, tm, D_MODEL), lambda bi, i: (bi, i, 0)),
                  pl.BlockSpec((None, 1, D_MODEL), lambda bi, i: (layer, 0, 0)),
                  mspec(k_sc), mspec(k_sh),
                  pl.BlockSpec((None, D_MODEL, LANES), lambda bi, i: (layer, 0, 0)),
                  pl.BlockSpec((None, 1, LANES), lambda bi, i: (layer, 0, 0))],
        out_specs=[pl.BlockSpec((None, tm, D_MODEL // 2), lambda bi, i: (bi, i, 0)),
                   pl.BlockSpec((None, tm, LANES), lambda bi, i: (bi, i, 0)),
                   pl.BlockSpec((None, tm, LANES), lambda bi, i: (bi, i, 0))],
        out_shape=[jax.ShapeDtypeStruct((b, l, D_MODEL // 2), jnp.uint32),
                   jax.ShapeDtypeStruct((b, l, LANES), I32),
                   jax.ShapeDtypeStruct((b, l, LANES), F32)],
        compiler_params=_params("arbitrary", "arbitrary"),
        name="router",
    )(x3, norm_w3, mod, mod, router_wp, router_bp)


def _row_copy(src_ref, src_row, dst_ref, dst_row, sem):
    return pltpu.make_async_copy(src_ref.at[pl.ds(src_row, 1)], dst_ref.at[pl.ds(dst_row, 1)], sem)


def _dispatch_kernel(slot_ref, h_ref, xs_in_ref, o_ref, sem):
    del xs_in_ref
    tt = h_ref.shape[0]

    def issue(r, carry):
        for k in range(TOP_K):
            _row_copy(h_ref, r, o_ref, slot_ref[0, r * TOP_K + k], sem).start(priority=k % 2)
        return carry

    lax.fori_loop(0, tt, issue, 0)

    def drain(r, carry):
        for k in range(TOP_K):
            _row_copy(h_ref, 0, o_ref, 0, sem).wait()
        return carry

    lax.fori_loop(0, tt, drain, 0)


def _dispatch(slots3, h_rows, xs):
    nsteps, _, w = slots3.shape
    tt = w // TOP_K
    return pl.pallas_call(
        _dispatch_kernel,
        grid=(nsteps,),
        in_specs=[pl.BlockSpec((None, 1, w), lambda i: (i, 0, 0), memory_space=pltpu.SMEM),
                  pl.BlockSpec((tt, h_rows.shape[1]), lambda i: (i, 0)),
                  pl.BlockSpec(memory_space=pl.ANY)],
        out_specs=pl.BlockSpec(memory_space=pl.ANY),
        out_shape=jax.ShapeDtypeStruct(xs.shape, xs.dtype),
        scratch_shapes=[pltpu.SemaphoreType.DMA(())],
        input_output_aliases={2: 0},
        compiler_params=_params("arbitrary"),
        name="moe_dispatch",
    )(slots3, h_rows, xs)


def _run_start(te_ref, i):
    return (i == 0) | (te_ref[i] != te_ref[jnp.maximum(i - 1, 0)])


def _weight_copy(w_ref, e, col, tn, buf_ref, sem_ref, part):
    return pltpu.make_async_copy(w_ref.at[e, :, pl.ds(pl.multiple_of(col, LANES), tn)],
                                 buf_ref.at[part], sem_ref.at[part])


CAST_ROWS = 512


def _cast_rows(src_ref, dst_ref):
    parts, rows, _ = src_ref.shape

    def body(c, carry):
        band = pl.ds(pl.multiple_of(c * CAST_ROWS, CAST_ROWS), CAST_ROWS)
        for p in range(parts):
            dst_ref[p, band, :] = src_ref[p, band, :].astype(BF)
        return carry

    lax.fori_loop(0, rows // CAST_ROWS, body, 0)


def _stream_weights(te_ref, nv_ref, nxt_ref, copies, cast):
    j = pl.program_id(0)
    i = pl.program_id(1)
    e = te_ref[i]

    @pl.when((i < nv_ref[0]) & _run_start(te_ref, i))
    def _():
        @pl.when((j == 0) & (i == 0))
        def _():
            for cp in copies(e, j):
                cp.start(priority=WEIGHT_DMA_PRIORITY)

        for cp in copies(e, j):
            cp.wait()
        cast()
        nxt = nxt_ref[e]
        jn = j + jnp.where(nxt <= e, 1, 0)

        @pl.when(jn < pl.num_programs(0))
        def _():
            for cp in copies(nxt, jn):
                cp.start(priority=WEIGHT_DMA_PRIORITY)


def _moe_up_kernel(te_ref, nv_ref, nxt_ref, x_ref, w_ref, bg_ref, bu_ref, o_ref, wbuf_ref, wbf_ref, sem_ref, *, e0):
    i = pl.program_id(1)
    valid = i < nv_ref[0]
    tn = o_ref.shape[1]
    half = x_ref.shape[1]

    def copies(e, j):
        return [_weight_copy(w_ref, e0 + e, part * EXPERT_DIM + j * tn, tn, wbuf_ref, sem_ref, part)
                for part in range(2)]

    def cast():
        _cast_rows(wbuf_ref, wbf_ref)

    _stream_weights(te_ref, nv_ref, nxt_ref, copies, cast)

    @pl.when(valid)
    def _():
        lo, hi = _unpack_bf16_pairs(x_ref[...])

        def proj(part):
            return (jnp.dot(lo, wbf_ref[part, :half, :], preferred_element_type=F32)
                    + jnp.dot(hi, wbf_ref[part, half:, :], preferred_element_type=F32))

        g = proj(0) + bg_ref[...]
        u = proj(1) + bu_ref[...]
        g = jnp.minimum(g, SWIGLU_LIMIT)
        u = jnp.clip(u, -SWIGLU_LIMIT, SWIGLU_LIMIT)
        o_ref[...] = (g * jax.nn.sigmoid(SWIGLU_ALPHA * g) * (u + 1.0)).astype(o_ref.dtype)

    @pl.when(jnp.logical_not(valid))
    def _():
        o_ref[...] = jnp.zeros(o_ref.shape, o_ref.dtype)


def _moe_up(tile_e, n_valid, next_e, xs, w_gu, b_gu3, layer):
    ns, half = xs.shape
    t, tn = MOE_TILE, MOE_UP_TN
    nt = ns // t
    nj = EXPERT_DIM // tn
    e0 = layer * N_EXPERTS

    def row(i, nv):
        return jnp.minimum(i, nv[0] - 1)

    grid_spec = pltpu.PrefetchScalarGridSpec(
        num_scalar_prefetch=3,
        grid=(nj, nt),
        in_specs=[pl.BlockSpec((t, half), lambda j, i, te, nv, nx: (row(i, nv), 0)),
                  pl.BlockSpec(memory_space=pl.ANY),
                  pl.BlockSpec((None, 1, tn), lambda j, i, te, nv, nx: (e0 + te[i], 0, j)),
                  pl.BlockSpec((None, 1, tn), lambda j, i, te, nv, nx: (e0 + te[i], 0, nj + j))],
        out_specs=pl.BlockSpec((t, tn), lambda j, i, te, nv, nx: (i, j)),
        scratch_shapes=[pltpu.VMEM((2, D_MODEL, tn), F32), pltpu.VMEM((2, D_MODEL, tn), BF),
                        pltpu.SemaphoreType.DMA((2,))],
    )
    return pl.pallas_call(
        functools.partial(_moe_up_kernel, e0=e0),
        grid_spec=grid_spec,
        out_shape=jax.ShapeDtypeStruct((ns, EXPERT_DIM), BF),
        compiler_params=_params("arbitrary", "arbitrary"),
        name="moe_up",
    )(tile_e, n_valid, next_e, xs, w_gu, b_gu3, b_gu3)


def _moe_down_kernel(te_ref, nv_ref, nxt_ref, a_ref, w_ref, b_ref, o_ref, wbuf_ref, wbf_ref, sem_ref, *, e0):
    i = pl.program_id(1)
    valid = i < nv_ref[0]
    tn = o_ref.shape[1]

    def copies(e, j):
        return [_weight_copy(w_ref, e0 + e, j * tn, tn, wbuf_ref, sem_ref, 0)]

    def cast():
        _cast_rows(wbuf_ref, wbf_ref)

    _stream_weights(te_ref, nv_ref, nxt_ref, copies, cast)

    @pl.when(valid)
    def _():
        o_ref[...] = jnp.dot(a_ref[...], wbf_ref[0], preferred_element_type=F32) + b_ref[...]

    @pl.when(jnp.logical_not(valid))
    def _():
        o_ref[...] = jnp.zeros(o_ref.shape, o_ref.dtype)


def _moe_down(tile_e, n_valid, next_e, act, w_down, b_down3, layer):
    ns = act.shape[0]
    t, tn = MOE_TILE, MOE_DOWN_TN
    nt = ns // t
    nj = D_MODEL // tn
    e0 = layer * N_EXPERTS

    def row(i, nv):
        return jnp.minimum(i, nv[0] - 1)

    grid_spec = pltpu.PrefetchScalarGridSpec(
        num_scalar_prefetch=3,
        grid=(nj, nt),
        in_specs=[pl.BlockSpec((t, EXPERT_DIM), lambda j, i, te, nv, nx: (row(i, nv), 0)),
                  pl.BlockSpec(memory_space=pl.ANY),
                  pl.BlockSpec((None, 1, tn), lambda j, i, te, nv, nx: (e0 + te[i], 0, j))],
        out_specs=pl.BlockSpec((t, tn), lambda j, i, te, nv, nx: (i, j)),
        scratch_shapes=[pltpu.VMEM((1, EXPERT_DIM, tn), F32), pltpu.VMEM((1, EXPERT_DIM, tn), BF),
                        pltpu.SemaphoreType.DMA((1,))],
    )
    return pl.pallas_call(
        functools.partial(_moe_down_kernel, e0=e0),
        grid_spec=grid_spec,
        out_shape=jax.ShapeDtypeStruct((ns, D_MODEL), F32),
        compiler_params=_params("arbitrary", "arbitrary"),
        name="moe_down",
    )(tile_e, n_valid, next_e, act, w_down, b_down3)


def _combine_kernel(slot_ref, y_ref, gate_ref, x_ref, g_ref, o_ref, buf_ref, sem):
    tt = x_ref.shape[0]

    def issue(r, carry):
        for k in range(TOP_K):
            _row_copy(y_ref, slot_ref[0, r * TOP_K + k], buf_ref.at[k], r, sem).start(priority=k % 2)
        return carry

    lax.fori_loop(0, tt, issue, 0)

    def drain(r, carry):
        for k in range(TOP_K):
            _row_copy(y_ref, 0, buf_ref.at[k], r, sem).wait()
        return carry

    lax.fori_loop(0, tt, drain, 0)
    gates = gate_ref[...]
    acc = gates[:, 0:1] * buf_ref[0]
    for k in range(1, TOP_K):
        acc = acc + gates[:, k:k + 1] * buf_ref[k]
    o_ref[...] = x_ref[...] + g_ref[...] * acc


def _combine(slots3, y_slots, gates3, x3, mod, k_gate, tt):
    b, l, _ = x3.shape
    per_tok = mod.shape[1] > 1
    lt = l // tt
    mblk = (None, tt if per_tok else 1, D_MODEL)
    return pl.pallas_call(
        _combine_kernel,
        grid=(b, lt),
        in_specs=[pl.BlockSpec((None, 1, tt * TOP_K), lambda bi, i: (bi * lt + i, 0, 0), memory_space=pltpu.SMEM),
                  pl.BlockSpec(memory_space=pl.ANY),
                  pl.BlockSpec((None, tt, LANES), lambda bi, i: (bi, i, 0)),
                  pl.BlockSpec((None, tt, D_MODEL), lambda bi, i: (bi, i, 0)),
                  pl.BlockSpec(mblk, lambda bi, i: (bi, i if per_tok else 0, k_gate))],
        out_specs=pl.BlockSpec((None, tt, D_MODEL), lambda bi, i: (bi, i, 0)),
        out_shape=jax.ShapeDtypeStruct((b, l, D_MODEL), F32),
        scratch_shapes=[pltpu.VMEM((TOP_K, tt, D_MODEL), F32), pltpu.SemaphoreType.DMA(())],
        compiler_params=_params("arbitrary", "arbitrary"),
        name="moe_combine",
    )(slots3, y_slots, gates3, x3, mod)


def _routing_tables(top_idx, n_tokens):
    m = n_tokens * TOP_K
    t = MOE_TILE
    flat_e = top_idx.reshape(m)
    onehot = (flat_e[:, None] == jnp.arange(N_EXPERTS, dtype=I32)[None, :]).astype(I32)
    csum = jnp.cumsum(onehot, axis=0)
    pos = jnp.sum(csum * onehot, axis=1) - 1
    counts = csum[-1]
    tiles_e = (counts + t - 1) // t
    tile_end = jnp.cumsum(tiles_e)
    tile_start = tile_end - tiles_e
    slot = tile_start[flat_e] * t + pos
    n_tiles = m // t + N_EXPERTS
    n_valid = tile_end[-1]
    tile_ids = jnp.minimum(jnp.arange(n_tiles, dtype=I32), n_valid - 1)
    tile_e = jnp.minimum(jnp.sum((tile_end[None, :] <= tile_ids[:, None]).astype(I32), axis=1), N_EXPERTS - 1)
    ids = jnp.arange(N_EXPERTS, dtype=I32)
    nonempty = tiles_e > 0
    later = jnp.min(jnp.where(nonempty[None, :] & (ids[None, :] > ids[:, None]), ids[None, :], N_EXPERTS), axis=1)
    first = jnp.min(jnp.where(nonempty, ids, N_EXPERTS))
    next_e = jnp.where(later < N_EXPERTS, later, first)
    return slot.astype(I32), n_tiles * t, tile_e.astype(I32), n_valid.astype(I32).reshape(1), next_e.astype(I32)


def _mixing(xr, mod, b, l, pos, k_past, v_past, conv_state, ssm_state, weights, layer, tm):
    (norm_mix_w3, w_in, w_dt3, q_norm_w3, k_norm_w3, ssd_consts, w_out) = weights
    m = b * l
    h = _norm_mod(xr, norm_mix_w3, layer, mod, 0, 1, tm).reshape(m, D_MODEL)
    qkv = _matmul(h, w_in, layer, 0, QKV_WIDTH // 512, 512, tm)
    z = _matmul(h, w_in, layer, QKV_WIDTH // 512, SSD_WIDTH // 512, 512, tm)
    xbc = _matmul(h, w_in, layer, (QKV_WIDTH + SSD_WIDTH) // 512, CONV_DIM // 512, 512, tm)
    dtr = _matmul(h, w_dt3, layer, 0, 1, LANES, tm)

    cosf, sinf = _rope_tables(jnp.tile(pos, b))
    qn, kn = _qk_norm_rope(qkv, cosf, sinf, q_norm_w3, k_norm_w3, layer, tm)
    v = qkv[:, ATTN_WIDTH + KV_WIDTH:]

    if k_past is None:
        attn3 = _attn_prompt(qn.reshape(b, l, ATTN_WIDTH), kn.reshape(b, l, KV_WIDTH),
                             qkv.reshape(b, l, QKV_WIDTH))
        xbc3, z3, dtr3 = (a.reshape(b, l, -1) for a in (xbc, z, dtr))
        valid = SSD_CHUNK
    else:
        rows = Q_PER_KV * l
        qg = qn.reshape(b, l, KV_HEADS, Q_PER_KV, HEAD_DIM).transpose(0, 2, 3, 1, 4).reshape(b, KV_HEADS, rows, HEAD_DIM)
        padn = ((0, 0), (0, LANES - l), (0, 0))
        og = _attn_sample(qg, k_past, v_past, layer, jnp.pad(kn.reshape(b, l, KV_WIDTH), padn),
                          jnp.pad(v.reshape(b, l, KV_WIDTH), padn), l)
        attn3 = og.reshape(b, KV_HEADS, Q_PER_KV, l, HEAD_DIM).transpose(0, 3, 1, 2, 4).reshape(b, l, ATTN_WIDTH)
        attn3 = attn3.astype(BF)
        padc = ((0, 0), (0, SSD_CHUNK - l), (0, 0))
        xbc3, z3, dtr3 = (jnp.pad(a.reshape(b, l, -1), padc) for a in (xbc, z, dtr))
        valid = l

    conv_st8 = jnp.pad(conv_state, ((0, 0), (SUBLANES - (CONV_WIDTH - 1), 0), (0, 0)))
    ssd3, h_fin = _ssd(xbc3, z3, dtr3, conv_st8, ssm_state.reshape(b, SSD_HEADS * SSD_HEAD_DIM, SSD_STATE),
                       ssd_consts, layer, valid)
    ssd3 = ssd3[:, :l]
    x1 = _outproj(attn3.reshape(xr.shape[0], xr.shape[1], ATTN_WIDTH), ssd3.reshape(xr.shape[0], xr.shape[1], SSD_WIDTH),
                  w_out, layer, xr, mod, 2, tm, 512)
    keep = CONV_WIDTH - 1
    assert l >= keep
    new_conv = xbc.reshape(b, l, CONV_DIM)[:, l - keep:]
    return (x1, kn.reshape(b, l, KV_HEADS, HEAD_DIM), v.reshape(b, l, KV_HEADS, HEAD_DIM), new_conv,
            h_fin.reshape(b, SSD_HEADS, SSD_HEAD_DIM, SSD_STATE))


def kernel(x_prompt, x_sample, cache_k, cache_v, state_conv, state_ssm, c_prompt, c_sample, w_ada, b_ada, norm_mix_w, norm_ffn_w, w_in, q_norm_w, k_norm_w, conv_w, conv_b, dt_bias, a_log, d_skip, ssd_norm_w, w_out, router_w, router_b, w_gate_up, b_gate_up, w_down, b_down):
    depth = w_ada.shape[0]
    bp, seq, _ = x_prompt.shape
    bs, dseq, _ = x_sample.shape
    kv_win = cache_k.shape[2]
    assert kv_win == MAX_REACH and PAST_LEN >= MAX_REACH
    n_p, n_s = bp * seq, bs * dseq
    n_tok = n_p + n_s
    keep_p = min(MAX_REACH, seq)

    vec3 = lambda a: a.reshape(depth, 1, -1)
    pad_lanes = lambda a: jnp.pad(a, [(0, 0)] * (a.ndim - 1) + [(0, LANES - a.shape[-1])])
    b_ada3 = vec3(b_ada)
    norm_mix_w3, norm_ffn_w3 = vec3(norm_mix_w), vec3(norm_ffn_w)
    w_dt3 = pad_lanes(w_in[:, :, DT_COL0:])
    expand = (jnp.arange(LANES, dtype=I32)[:, None]
              == jnp.arange(SSD_WIDTH, dtype=I32)[None, :] // SSD_HEAD_DIM).astype(F32)
    ssd_consts = (conv_w, vec3(conv_b), vec3(pad_lanes(dt_bias)), vec3(pad_lanes(a_log)),
                  vec3(jnp.repeat(d_skip, SSD_HEAD_DIM, axis=-1)), vec3(ssd_norm_w), expand)
    weights = (norm_mix_w3, w_in, w_dt3, vec3(q_norm_w), vec3(k_norm_w), ssd_consts,
               w_out.reshape(depth, 2, MIX_WIDTH // 2, D_MODEL).reshape(depth * 2, MIX_WIDTH // 2, D_MODEL))
    router_wp = pad_lanes(router_w)
    router_bp = vec3(pad_lanes(router_b))
    w_gu = w_gate_up.reshape(depth * N_EXPERTS, D_MODEL, 2 * EXPERT_DIM)
    b_gu3 = b_gate_up.reshape(depth * N_EXPERTS, 1, 2 * EXPERT_DIM)
    w_dn = w_down.reshape(depth * N_EXPERTS, EXPERT_DIM, D_MODEL)
    b_dn3 = b_down.reshape(depth * N_EXPERTS, 1, D_MODEL)

    n_c = bp + bs
    c_all = jnp.concatenate([c_prompt, c_sample, jnp.zeros((-n_c % SUBLANES, D_MODEL), F32)], axis=0)
    pos_p = jnp.arange(seq, dtype=I32)
    pos_s = PAST_LEN + jnp.arange(dseq, dtype=I32)

    y_p = x_prompt
    y_s = x_sample.reshape(1, n_s, D_MODEL)
    outs = [[] for _ in range(8)]
    for layer in range(depth):
        mod = _ada(c_all, w_ada, b_ada3, layer)
        mod_p = mod[:bp].reshape(bp, 1, 6 * D_MODEL)
        mod_s = jnp.repeat(mod[bp:n_c], dseq, axis=0).reshape(1, n_s, 6 * D_MODEL)

        zero_conv = jnp.zeros((bp, CONV_WIDTH - 1, CONV_DIM), F32)
        zero_ssm = jnp.zeros((bp, SSD_HEADS, SSD_HEAD_DIM, SSD_STATE), F32)
        x1_p, kp, vp, cp, sp = _mixing(y_p, mod_p, bp, seq, pos_p, None, None, zero_conv, zero_ssm,
                                       weights, layer, 512)
        x1_s, kn, vn, cn, sn = _mixing(y_s, mod_s, bs, dseq, pos_s, cache_k, cache_v, state_conv[layer],
                                       state_ssm[layer], weights, layer, n_s)

        h2_p, idx_p, gate_p = _router(x1_p, norm_ffn_w3, layer, mod_p, 3, 4, router_wp, router_bp, 256)
        h2_s, idx_s, gate_s = _router(x1_s, norm_ffn_w3, layer, mod_s, 3, 4, router_wp, router_bp, n_s)
        top_idx = jnp.concatenate([idx_p.reshape(n_p, LANES)[:, :TOP_K], idx_s.reshape(n_s, LANES)[:, :TOP_K]], axis=0)
        slot, n_slots, tile_e, n_valid, next_e = _routing_tables(top_idx, n_tok)
        slot_p, slot_s = slot[:n_p * TOP_K], slot[n_p * TOP_K:]
        td = 256
        xs = jnp.zeros((n_slots, D_MODEL // 2), jnp.uint32)
        xs = _dispatch(slot_p.reshape(n_p // td, 1, td * TOP_K), h2_p.reshape(n_p, D_MODEL // 2), xs)
        xs = _dispatch(slot_s.reshape(1, 1, n_s * TOP_K), h2_s.reshape(n_s, D_MODEL // 2), xs)
        act = _moe_up(tile_e, n_valid, next_e, xs, w_gu, b_gu3, layer)
        y_slots = _moe_down(tile_e, n_valid, next_e, act, w_dn, b_dn3, layer)
        tt = 128
        y_p = _combine(slot_p.reshape(n_p // tt, 1, tt * TOP_K), y_slots, gate_p, x1_p, mod_p, 5, tt)
        y_s = _combine(slot_s.reshape(n_s // tt, 1, tt * TOP_K), y_slots, gate_s, x1_s, mod_s, 5, tt)

        for lst, val in zip(outs, (kp[:, seq - keep_p:], vp[:, seq - keep_p:], cp, sp, kn, vn, cn, sn)):
            lst.append(val)
    return (y_p, y_s.reshape(bs, dseq, D_MODEL)) + tuple(jnp.stack(o) for o in outs)
```

```python
import functools
import math

import jax
import jax.numpy as jnp
from jax import lax
from jax.experimental import pallas as pl
from jax.experimental.pallas import tpu as pltpu

F32 = jnp.float32
BF = jnp.bfloat16
I32 = jnp.int32

D_MODEL = 4096
PAST_LEN = 8192
HEAD_DIM = 128
ATTN_HEADS = 16
KV_HEADS = 4
Q_PER_KV = ATTN_HEADS // KV_HEADS
ATTN_WIDTH = ATTN_HEADS * HEAD_DIM
KV_WIDTH = KV_HEADS * HEAD_DIM
ROT_DIM = HEAD_DIM // 4
ROPE_THETA = 500000.0
DILATED_BRANCHES = ((128, 1), (512, 4), (2048, 16))
MAX_REACH = 2048
SSD_WIDTH = 2048
SSD_HEAD_DIM = 64
SSD_HEADS = 32
SSD_GROUPS = 8
SSD_STATE = 128
CONV_WIDTH = 4
CONV_DIM = SSD_WIDTH + 2 * SSD_GROUPS * SSD_STATE
SSD_CHUNK = 128
MIX_WIDTH = ATTN_WIDTH + SSD_WIDTH
N_EXPERTS = 32
TOP_K = 4
EXPERT_DIM = D_MODEL
SWIGLU_LIMIT = 7.0
SWIGLU_ALPHA = 1.702
EPS = 1e-6

LANES = 128
SUBLANES = 8
NEG = -1e30
QKV_WIDTH = ATTN_WIDTH + 2 * KV_WIDTH
DT_COL0 = QKV_WIDTH + SSD_WIDTH + CONV_DIM
MOE_TILE = 256
MOE_UP_TN = 512
MOE_DOWN_TN = 1024
VMEM_LIMIT = 56 * 2 ** 20
WEIGHT_DMA_PRIORITY = 1
HI = lax.Precision.HIGHEST
NT_DIMS = (((1,), (1,)), ((), ()))
TN_DIMS = (((0,), (0,)), ((), ()))


def _params(*sem):
    return pltpu.CompilerParams(dimension_semantics=sem, vmem_limit_bytes=VMEM_LIMIT)


def _silu(x):
    return x * jax.nn.sigmoid(x)


def _ada_kernel(c_ref, w_ref, b_ref, o_ref):
    a = _silu(c_ref[...]).astype(BF)
    o_ref[...] = jnp.dot(a, w_ref[...].astype(BF), preferred_element_type=F32) + b_ref[...]


def _ada(c, w_ada, b_ada3, layer):
    mc = c.shape[0]
    n = 6 * D_MODEL
    tn = 512
    return pl.pallas_call(
        _ada_kernel,
        grid=(n // tn,),
        in_specs=[pl.BlockSpec((mc, D_MODEL), lambda j: (0, 0)),
                  pl.BlockSpec((None, D_MODEL, tn), lambda j: (layer, 0, j)),
                  pl.BlockSpec((None, 1, tn), lambda j: (layer, 0, j))],
        out_specs=pl.BlockSpec((mc, tn), lambda j: (0, j)),
        out_shape=jax.ShapeDtypeStruct((mc, n), F32),
        compiler_params=_params("arbitrary"),
        name="ada",
    )(c, w_ada, b_ada3)


def _norm_mod_kernel(x_ref, w_ref, sc_ref, sh_ref, o_ref):
    x = x_ref[...]
    y = x * lax.rsqrt(jnp.mean(x * x, axis=-1, keepdims=True) + EPS) * w_ref[...]
    o_ref[...] = (y * (1.0 + sc_ref[...]) + sh_ref[...]).astype(o_ref.dtype)


def _norm_mod(x3, norm_w3, layer, mod, k_sh, k_sc, tm):
    b, l, _ = x3.shape
    per_tok = mod.shape[1] > 1
    mblk = (None, tm if per_tok else 1, D_MODEL)

    def mspec(k):
        return pl.BlockSpec(mblk, lambda bi, i: (bi, i if per_tok else 0, k))

    return pl.pallas_call(
        _norm_mod_kernel,
        grid=(b, l // tm),
        in_specs=[pl.BlockSpec((None, tm, D_MODEL), lambda bi, i: (bi, i, 0)),
                  pl.BlockSpec((None, 1, D_MODEL), lambda bi, i: (layer, 0, 0)),
                  mspec(k_sc), mspec(k_sh)],
        out_specs=pl.BlockSpec((None, tm, D_MODEL), lambda bi, i: (bi, i, 0)),
        out_shape=jax.ShapeDtypeStruct((b, l, D_MODEL), BF),
        compiler_params=_params("arbitrary", "arbitrary"),
        name="norm_mod",
    )(x3, norm_w3, mod, mod)


def _mm_kernel(a_ref, w_ref, o_ref, wb_ref):
    @pl.when(pl.program_id(1) == 0)
    def _():
        wb_ref[...] = w_ref[...].astype(BF)

    o_ref[...] = jnp.dot(a_ref[...], wb_ref[...], preferred_element_type=F32)


def _matmul(a, w3, layer, col_blk0, n_tiles, tn, tm):
    m, k = a.shape
    return pl.pallas_call(
        _mm_kernel,
        grid=(n_tiles, m // tm),
        in_specs=[pl.BlockSpec((tm, k), lambda j, i: (i, 0)),
                  pl.BlockSpec((None, k, tn), lambda j, i: (layer, 0, col_blk0 + j))],
        out_specs=pl.BlockSpec((tm, tn), lambda j, i: (i, j)),
        out_shape=jax.ShapeDtypeStruct((m, n_tiles * tn), F32),
        scratch_shapes=[pltpu.VMEM((k, tn), BF)],
        compiler_params=_params("arbitrary", "arbitrary"),
        name="matmul",
    )(a, w3)


def _rope_tables(pos):
    half = ROT_DIM // 2
    inv_freq = jnp.power(ROPE_THETA, -jnp.arange(half, dtype=F32) * 2.0 / ROT_DIM)
    ang = pos.astype(F32)[:, None] * inv_freq[None, :]
    cos, sin = jnp.cos(ang), jnp.sin(ang)
    n = pos.shape[0]
    cosf = jnp.concatenate([cos, cos, jnp.ones((n, HEAD_DIM - ROT_DIM), F32)], axis=1)
    sinf = jnp.concatenate([-sin, sin, jnp.zeros((n, HEAD_DIM - ROT_DIM), F32)], axis=1)
    return cosf, sinf


def _norm_rope_head(xh, w, cosf, sinf, lane):
    half = ROT_DIM // 2
    y = xh * lax.rsqrt(jnp.mean(xh * xh, axis=-1, keepdims=True) + EPS) * w
    partner = jnp.where(lane < half, pltpu.roll(y, HEAD_DIM - half, 1), pltpu.roll(y, half, 1))
    return y * cosf + partner * sinf


def _qk_kernel(q_ref, k_ref, cos_ref, sin_ref, qw_ref, kw_ref, qo_ref, ko_ref):
    cosf = cos_ref[...]
    sinf = sin_ref[...]
    lane = lax.broadcasted_iota(I32, cosf.shape, 1)
    scale = HEAD_DIM ** -0.5
    for h in range(ATTN_HEADS):
        sl = slice(h * HEAD_DIM, (h + 1) * HEAD_DIM)
        qo_ref[:, sl] = _norm_rope_head(q_ref[:, sl], qw_ref[...], cosf, sinf, lane) * scale
    for h in range(KV_HEADS):
        sl = slice(h * HEAD_DIM, (h + 1) * HEAD_DIM)
        ko_ref[:, sl] = _norm_rope_head(k_ref[:, sl], kw_ref[...], cosf, sinf, lane)


def _qk_norm_rope(qkv, cosf, sinf, q_norm_w3, k_norm_w3, layer, tm):
    m = qkv.shape[0]
    return pl.pallas_call(
        _qk_kernel,
        grid=(m // tm,),
        in_specs=[pl.BlockSpec((tm, ATTN_WIDTH), lambda i: (i, 0)),
                  pl.BlockSpec((tm, KV_WIDTH), lambda i: (i, ATTN_WIDTH // KV_WIDTH)),
                  pl.BlockSpec((tm, HEAD_DIM), lambda i: (i, 0)),
                  pl.BlockSpec((tm, HEAD_DIM), lambda i: (i, 0)),
                  pl.BlockSpec((None, 1, HEAD_DIM), lambda i: (layer, 0, 0)),
                  pl.BlockSpec((None, 1, HEAD_DIM), lambda i: (layer, 0, 0))],
        out_specs=[pl.BlockSpec((tm, ATTN_WIDTH), lambda i: (i, 0)),
                   pl.BlockSpec((tm, KV_WIDTH), lambda i: (i, 0))],
        out_shape=[jax.ShapeDtypeStruct((m, ATTN_WIDTH), F32),
                   jax.ShapeDtypeStruct((m, KV_WIDTH), F32)],
        compiler_params=_params("arbitrary"),
        name="qk_norm_rope",
    )(qkv, qkv, cosf, sinf, q_norm_w3, k_norm_w3)


def _multiplicity(delta):
    ge0 = delta >= 0
    cnt = jnp.zeros(delta.shape, F32)
    for window, dil in DILATED_BRANCHES:
        hit = ge0 & (delta <= window) & ((delta & (dil - 1)) == 0)
        cnt = cnt + jnp.where(hit, 1.0, 0.0)
    return cnt


ATTN_QB = 256
ATTN_KB = 256


def _attn_prompt_kernel(q_ref, k_ref, v_ref, tab_ref, o_ref):
    c = pl.program_id(2)
    qb = q_ref.shape[0]
    kb = tab_ref.shape[1]
    per = qb // kb
    q = q_ref[...]
    qs = jnp.concatenate([q[:, h * HEAD_DIM:(h + 1) * HEAD_DIM] for h in range(Q_PER_KV)], axis=0).astype(BF)
    last_kb = per * c + per - 1
    n_it = jnp.minimum(last_kb + 1, tab_ref.shape[0])

    def body(t, carry):
        m, l, acc = carry
        start = pl.multiple_of((last_kb - t) * kb, kb)
        kblk = k_ref[pl.ds(start, kb), :].astype(BF)
        vblk = v_ref[pl.ds(start, kb), :].astype(BF)
        cnt = jnp.concatenate([tab_ref[t]] * Q_PER_KV, axis=1)
        s = lax.dot_general(kblk, qs, NT_DIMS, preferred_element_type=F32)
        sm = jnp.where(cnt > 0.0, s, NEG)
        m_new = jnp.maximum(m, jnp.max(sm, axis=0, keepdims=True))
        p = cnt * jnp.exp(sm - m_new)
        alpha = jnp.exp(m - m_new)
        l = alpha * l + jnp.sum(p, axis=0, keepdims=True)
        acc = alpha * acc + lax.dot_general(vblk, p.astype(BF), TN_DIMS, preferred_element_type=F32)
        return m_new, l, acc

    cols = Q_PER_KV * qb
    init = (jnp.full((1, cols), NEG, F32), jnp.zeros((1, cols), F32), jnp.zeros((HEAD_DIM, cols), F32))
    _, l, acc = lax.fori_loop(0, n_it, body, init)
    out_t = acc / l
    for h in range(Q_PER_KV):
        o_ref[:, h * HEAD_DIM:(h + 1) * HEAD_DIM] = out_t[:, h * qb:(h + 1) * qb].T.astype(o_ref.dtype)


def _attn_prompt(qn3, kn3, qkv3):
    b, l, _ = qn3.shape
    qb, kb = ATTN_QB, ATTN_KB
    assert l % qb == 0
    gw = Q_PER_KV * HEAD_DIM
    v_blk0 = (ATTN_WIDTH + KV_WIDTH) // HEAD_DIM
    n_tab = MAX_REACH // kb + qb // kb
    tt = jnp.arange(n_tab, dtype=I32)[:, None, None]
    jj = jnp.arange(kb, dtype=I32)[None, :, None]
    ii = jnp.arange(qb, dtype=I32)[None, None, :]
    tables = _multiplicity((tt - (qb // kb - 1)) * kb + ii - jj)
    return pl.pallas_call(
        _attn_prompt_kernel,
        grid=(b, KV_HEADS, l // qb),
        in_specs=[pl.BlockSpec((None, qb, gw), lambda bi, g, c: (bi, c, g)),
                  pl.BlockSpec((None, l, HEAD_DIM), lambda bi, g, c: (bi, 0, g)),
                  pl.BlockSpec((None, l, HEAD_DIM), lambda bi, g, c: (bi, 0, v_blk0 + g)),
                  pl.BlockSpec((n_tab, kb, qb), lambda bi, g, c: (0, 0, 0))],
        out_specs=pl.BlockSpec((None, qb, gw), lambda bi, g, c: (bi, c, g)),
        out_shape=jax.ShapeDtypeStruct((b, l, ATTN_WIDTH), BF),
        compiler_params=_params("arbitrary", "arbitrary", "arbitrary"),
        name="attn_prompt",
    )(qn3, kn3, qkv3, tables)


def _attn_sample_kernel(q_ref, ck_ref, cv_ref, kn_ref, vn_ref, o_ref, *, n_new):
    w = ck_ref.shape[0] // KV_HEADS
    pad = kn_ref.shape[0]
    rows = q_ref.shape[1]
    ri = lax.broadcasted_iota(I32, (rows, w + pad), 0) & (n_new - 1)
    cj = lax.broadcasted_iota(I32, (rows, w + pad), 1)
    cnt = _multiplicity(w + ri - cj)
    for g in range(KV_HEADS):
        sl = slice(g * HEAD_DIM, (g + 1) * HEAD_DIM)
        qg = q_ref[g].astype(BF)
        s_c = lax.dot_general(qg, ck_ref[pl.ds(g, w, stride=KV_HEADS), :].astype(BF), NT_DIMS,
                              preferred_element_type=F32)
        s_n = lax.dot_general(qg, kn_ref[:, sl].astype(BF), NT_DIMS, preferred_element_type=F32)
        s = jnp.concatenate([s_c, s_n], axis=1)
        sm = jnp.where(cnt > 0.0, s, NEG)
        m = jnp.max(sm, axis=-1, keepdims=True)
        p = cnt * jnp.exp(sm - m)
        l = jnp.sum(p, axis=-1, keepdims=True)
        pb = p.astype(BF)
        o = (jnp.dot(pb[:, :w], cv_ref[pl.ds(g, w, stride=KV_HEADS), :].astype(BF), preferred_element_type=F32)
             + jnp.dot(pb[:, w:], vn_ref[:, sl].astype(BF), preferred_element_type=F32))
        o_ref[g] = o / l


def _attn_sample(qg4, cache_k, cache_v, layer, knp3, vnp3, n_new):
    b, _, rows, _ = qg4.shape
    depth, _, w = cache_k.shape[:3]
    pad = knp3.shape[1]
    cache_k, cache_v = (a.reshape(depth, b, w * KV_HEADS, HEAD_DIM) for a in (cache_k, cache_v))
    cache_spec = pl.BlockSpec((None, None, w * KV_HEADS, HEAD_DIM), lambda i: (layer, i, 0, 0))
    return pl.pallas_call(
        functools.partial(_attn_sample_kernel, n_new=n_new),
        grid=(b,),
        in_specs=[pl.BlockSpec((None, KV_HEADS, rows, HEAD_DIM), lambda i: (i, 0, 0, 0)),
                  cache_spec, cache_spec,
                  pl.BlockSpec((None, pad, KV_WIDTH), lambda i: (i, 0, 0)),
                  pl.BlockSpec((None, pad, KV_WIDTH), lambda i: (i, 0, 0))],
        out_specs=pl.BlockSpec((None, KV_HEADS, rows, HEAD_DIM), lambda i: (i, 0, 0, 0)),
        out_shape=jax.ShapeDtypeStruct((b, KV_HEADS, rows, HEAD_DIM), F32),
        compiler_params=_params("arbitrary"),
        name="attn_sample",
    )(qg4, cache_k, cache_v, knp3, vnp3)


def _ssd_kernel(x_ref, z_ref, dtr_ref, cw_ref, cb_ref, dtb_ref, alog_ref, dsk_ref, nw_ref, cst_ref, sst_ref,
                e_ref, y_ref, hout_ref, h_ref, tail_ref, buf_ref, *, valid):
    lc = x_ref.shape[0]
    c = pl.program_id(1)
    gw = SSD_WIDTH // SSD_GROUPS
    hpg = SSD_HEADS // SSD_GROUPS
    b_col0 = SSD_WIDTH
    c_col0 = SSD_WIDTH + SSD_GROUPS * SSD_STATE

    @pl.when(c == 0)
    def _():
        h_ref[...] = sst_ref[...]
        tail_ref[...] = cst_ref[...]

    x = x_ref[...]
    buf_ref[0:SUBLANES, :] = tail_ref[...]
    buf_ref[SUBLANES:SUBLANES + lc, :] = x
    tail_ref[...] = x[lc - SUBLANES:lc, :]
    conv = cb_ref[...]
    for j in range(CONV_WIDTH):
        r0 = SUBLANES - (CONV_WIDTH - 1) + j
        conv = conv + cw_ref[j:j + 1, :] * buf_ref[r0:r0 + lc, :]
    u = _silu(conv)
    xs = u[:, :SSD_WIDTH]

    dtv = dtr_ref[...] + dtb_ref[...]
    dt = jnp.maximum(dtv, 0.0) + jnp.log1p(jnp.exp(-jnp.abs(dtv)))
    row = lax.broadcasted_iota(I32, (lc, lc), 0)
    col = lax.broadcasted_iota(I32, (lc, lc), 1)
    if valid < lc:
        dt = jnp.where(lax.broadcasted_iota(I32, dt.shape, 0) < valid, dt, 0.0)
    da = dt * (-jnp.exp(alog_ref[...]))
    causal = row >= col
    acs = jnp.dot(jnp.where(causal, 1.0, 0.0), da, precision=HI, preferred_element_type=F32)
    acs_t = acs.T
    last = acs[lc - 1:lc, :]
    expand = e_ref[...]
    dt_e = jnp.dot(dt, expand, precision=HI, preferred_element_type=F32)
    dfs_e = jnp.dot(jnp.exp(acs), expand, precision=HI, preferred_element_type=F32)
    dte_e = jnp.dot(jnp.exp(last - acs), expand, precision=HI, preferred_element_type=F32)
    xdt = xs * dt_e
    xdtw = xdt * dte_e
    lane = lax.broadcasted_iota(I32, (lc, LANES), 1)
    z = z_ref[...]

    for g in range(SSD_GROUPS):
        gs = slice(g * gw, (g + 1) * gw)
        bg = u[:, b_col0 + g * SSD_STATE:b_col0 + (g + 1) * SSD_STATE].astype(BF)
        cg = u[:, c_col0 + g * SSD_STATE:c_col0 + (g + 1) * SSD_STATE].astype(BF)
        cb = lax.dot_general(cg, bg, NT_DIMS, preferred_element_type=F32)
        hg = h_ref[g * gw:(g + 1) * gw, :]
        y_off = lax.dot_general(cg, hg.astype(BF), NT_DIMS, preferred_element_type=F32)
        pieces = []
        for pr in range(hpg // 2):
            xp = xdt[:, g * gw + pr * LANES:g * gw + (pr + 1) * LANES]
            acc = jnp.zeros((lc, LANES), F32)
            for hh in range(2):
                h = g * hpg + pr * 2 + hh
                diff = acs[:, h:h + 1] - acs_t[h:h + 1, :]
                decay = jnp.exp(jnp.where(causal, diff, -jnp.inf))
                mat = (cb * decay).astype(BF)
                keep = (lane < SSD_HEAD_DIM) if hh == 0 else (lane >= SSD_HEAD_DIM)
                acc = acc + jnp.dot(mat, jnp.where(keep, xp, 0.0).astype(BF), preferred_element_type=F32)
            pieces.append(acc)
        y_diag = jnp.concatenate(pieces, axis=1)
        yg = y_diag + y_off * dfs_e[:, gs] + xs[:, gs] * dsk_ref[:, gs]

        ds = lax.dot_general(xdtw[:, gs].astype(BF), bg, TN_DIMS, preferred_element_type=F32)
        for hh in range(hpg):
            h = g * hpg + hh
            hs = slice(hh * SSD_HEAD_DIM, (hh + 1) * SSD_HEAD_DIM)
            h_ref[g * gw + hh * SSD_HEAD_DIM:g * gw + (hh + 1) * SSD_HEAD_DIM, :] = (
                hg[hs] * jnp.exp(last[:, h:h + 1]) + ds[hs])

        yg = yg * _silu(z[:, gs])
        yg = yg * lax.rsqrt(jnp.mean(yg * yg, axis=-1, keepdims=True) + EPS)
        y_ref[:, gs] = (yg * nw_ref[:, gs]).astype(y_ref.dtype)

    @pl.when(c == pl.num_programs(1) - 1)
    def _():
        hout_ref[...] = h_ref[...]


def _ssd(xbc3, z3, dtr3, conv_st8, ssm_st3, ssd_consts, layer, valid):
    conv_w, conv_b3, dtb3, alog3, dsk3, nw3, expand = ssd_consts
    b, l, _ = xbc3.shape
    lc = min(SSD_CHUNK, l)
    vec = lambda wdt: pl.BlockSpec((None, 1, wdt), lambda bi, c: (layer, 0, 0))
    return pl.pallas_call(
        functools.partial(_ssd_kernel, valid=valid),
        grid=(b, l // lc),
        in_specs=[pl.BlockSpec((None, lc, CONV_DIM), lambda bi, c: (bi, c, 0)),
                  pl.BlockSpec((None, lc, SSD_WIDTH), lambda bi, c: (bi, c, 0)),
                  pl.BlockSpec((None, lc, LANES), lambda bi, c: (bi, c, 0)),
                  pl.BlockSpec((None, CONV_WIDTH, CONV_DIM), lambda bi, c: (layer, 0, 0)),
                  vec(CONV_DIM), vec(LANES), vec(LANES), vec(SSD_WIDTH), vec(SSD_WIDTH),
                  pl.BlockSpec((None, SUBLANES, CONV_DIM), lambda bi, c: (bi, 0, 0)),
                  pl.BlockSpec((None, SSD_HEADS * SSD_HEAD_DIM, SSD_STATE), lambda bi, c: (bi, 0, 0)),
                  pl.BlockSpec((LANES, SSD_WIDTH), lambda bi, c: (0, 0))],
        out_specs=[pl.BlockSpec((None, lc, SSD_WIDTH), lambda bi, c: (bi, c, 0)),
                   pl.BlockSpec((None, SSD_HEADS * SSD_HEAD_DIM, SSD_STATE), lambda bi, c: (bi, 0, 0))],
        out_shape=[jax.ShapeDtypeStruct((b, l, SSD_WIDTH), BF),
                   jax.ShapeDtypeStruct((b, SSD_HEADS * SSD_HEAD_DIM, SSD_STATE), F32)],
        scratch_shapes=[pltpu.VMEM((SSD_HEADS * SSD_HEAD_DIM, SSD_STATE), F32),
                        pltpu.VMEM((SUBLANES, CONV_DIM), F32),
                        pltpu.VMEM((SUBLANES + lc, CONV_DIM), F32)],
        compiler_params=_params("arbitrary", "arbitrary"),
        name="ssd",
    )(xbc3, z3, dtr3, conv_w, conv_b3, dtb3, alog3, dsk3, nw3, conv_st8, ssm_st3, expand)


def _outproj_kernel(a1_ref, a2_ref, w1_ref, w2_ref, x_ref, g_ref, o_ref, wb_ref):
    k1 = a1_ref.shape[1]

    @pl.when((pl.program_id(1) == 0) & (pl.program_id(2) == 0))
    def _():
        wb_ref[0:k1, :] = w1_ref[...].astype(BF)
        wb_ref[k1:, :] = w2_ref[...].astype(BF)

    acc = (jnp.dot(a1_ref[...], wb_ref[0:k1, :], preferred_element_type=F32)
           + jnp.dot(a2_ref[...], wb_ref[k1:, :], preferred_element_type=F32))
    o_ref[...] = x_ref[...] + g_ref[...] * acc


def _outproj(attn3, ssd3, w_out, layer, x3, mod, k_gate, tm, tn):
    b, l, k1 = attn3.shape
    k2 = ssd3.shape[2]
    per_tok = mod.shape[1] > 1
    nk = D_MODEL // tn
    mblk = (None, tm if per_tok else 1, tn)
    return pl.pallas_call(
        _outproj_kernel,
        grid=(nk, b, l // tm),
        in_specs=[pl.BlockSpec((None, tm, k1), lambda j, bi, i: (bi, i, 0)),
                  pl.BlockSpec((None, tm, k2), lambda j, bi, i: (bi, i, 0)),
                  pl.BlockSpec((None, k1, tn), lambda j, bi, i: (2 * layer, 0, j)),
                  pl.BlockSpec((None, k2, tn), lambda j, bi, i: (2 * layer + 1, 0, j)),
                  pl.BlockSpec((None, tm, tn), lambda j, bi, i: (bi, i, j)),
                  pl.BlockSpec(mblk, lambda j, bi, i: (bi, i if per_tok else 0, k_gate * nk + j))],
        out_specs=pl.BlockSpec((None, tm, tn), lambda j, bi, i: (bi, i, j)),
        out_shape=jax.ShapeDtypeStruct((b, l, D_MODEL), F32),
        scratch_shapes=[pltpu.VMEM((k1 + k2, tn), BF)],
        compiler_params=_params("arbitrary", "arbitrary", "arbitrary"),
        name="outproj",
    )(attn3, ssd3, w_out, w_out, x3, mod)


def _pack_bf16_pairs(h):
    n = h.shape[1] // 2
    bits = pltpu.bitcast(h.astype(BF).astype(F32), jnp.uint32)
    return bits[:, n:] | (bits[:, :n] >> 16)


def _unpack_bf16_pairs(w):
    lo = pltpu.bitcast(w << 16, F32).astype(BF)
    hi = pltpu.bitcast(w & jnp.uint32(0xFFFF0000), F32).astype(BF)
    return lo, hi


def _router_kernel(x_ref, w_ref, sc_ref, sh_ref, rw_ref, rb_ref, h_ref, idx_ref, gate_ref):
    x = x_ref[...]
    y = x * lax.rsqrt(jnp.mean(x * x, axis=-1, keepdims=True) + EPS) * w_ref[...]
    h = y * (1.0 + sc_ref[...]) + sh_ref[...]
    h_ref[...] = _pack_bf16_pairs(h)
    logits = jnp.dot(h, rw_ref[...], precision=HI, preferred_element_type=F32) + rb_ref[...]
    lane = lax.broadcasted_iota(I32, logits.shape, 1)
    lane_f = lane.astype(F32)
    work = jnp.where(lane < N_EXPERTS, logits, -jnp.inf)
    idx_out = jnp.zeros(logits.shape, F32)
    val_out = jnp.zeros(logits.shape, F32)
    top = None
    for k in range(TOP_K):
        m = jnp.max(work, axis=-1, keepdims=True)
        sel = jnp.min(jnp.where(work == m, lane_f, float(LANES)), axis=-1, keepdims=True)
        if k == 0:
            top = m
        idx_out = jnp.where(lane == k, sel, idx_out)
        val_out = jnp.where(lane == k, jnp.exp(m - top), val_out)
        work = jnp.where(lane_f == sel, -jnp.inf, work)
    idx_ref[...] = idx_out.astype(I32)
    gate_ref[...] = val_out / jnp.sum(val_out, axis=-1, keepdims=True)


def _router(x3, norm_w3, layer, mod, k_sh, k_sc, router_wp, router_bp, tm):
    b, l, _ = x3.shape
    per_tok = mod.shape[1] > 1
    mblk = (None, tm if per_tok else 1, D_MODEL)

    def mspec(k):
        return pl.BlockSpec(mblk, lambda bi, i: (bi, i if per_tok else 0, k))

    return pl.pallas_call(
        _router_kernel,
        grid=(b, l // tm),
        in_specs=[pl.BlockSpec((None, tm, D_MODEL), lambda bi, i: (bi, i, 0)),
                  pl.BlockSpec((None, 1, D_MODEL), lambda bi, i: (layer, 0, 0)),
                  mspec(k_sc), mspec(k_sh),
                  pl.BlockSpec((None, D_MODEL, LANES), lambda bi, i: (layer, 0, 0)),
                  pl.BlockSpec((None, 1, LANES), lambda bi, i: (layer, 0, 0))],
        out_specs=[pl.BlockSpec((None, tm, D_MODEL // 2), lambda bi, i: (bi, i, 0)),
                   pl.BlockSpec((None, tm, LANES), lambda bi, i: (bi, i, 0)),
                   pl.BlockSpec((None, tm, LANES), lambda bi, i: (bi, i, 0))],
        out_shape=[jax.ShapeDtypeStruct((b, l, D_MODEL // 2), jnp.uint32),
                   jax.ShapeDtypeStruct((b, l, LANES), I32),
                   jax.ShapeDtypeStruct((b, l, LANES), F32)],
        compiler_params=_params("arbitrary", "arbitrary"),
        name="router",
    )(x3, norm_w3, mod, mod, router_wp, router_bp)


def _row_copy(src_ref, src_row, dst_ref, dst_row, sem):
    return pltpu.make_async_copy(src_ref.at[pl.ds(src_row, 1)], dst_ref.at[pl.ds(dst_row, 1)], sem)


def _dispatch_kernel(slot_ref, h_ref, xs_in_ref, o_ref, sem):
    del xs_in_ref
    tt = h_ref.shape[0]

    def issue(r, carry):
        for k in range(TOP_K):
            _row_copy(h_ref, r, o_ref, slot_ref[0, r * TOP_K + k], sem).start(priority=k % 2)
        return carry

    lax.fori_loop(0, tt, issue, 0)

    def drain(r, carry):
        for k in range(TOP_K):
            _row_copy(h_ref, 0, o_ref, 0, sem).wait()
        return carry

    lax.fori_loop(0, tt, drain, 0)


def _dispatch(slots3, h_rows, xs):
    nsteps, _, w = slots3.shape
    tt = w // TOP_K
    return pl.pallas_call(
        _dispatch_kernel,
        grid=(nsteps,),
        in_specs=[pl.BlockSpec((None, 1, w), lambda i: (i, 0, 0), memory_space=pltpu.SMEM),
                  pl.BlockSpec((tt, h_rows.shape[1]), lambda i: (i, 0)),
                  pl.BlockSpec(memory_space=pl.ANY)],
        out_specs=pl.BlockSpec(memory_space=pl.ANY),
        out_shape=jax.ShapeDtypeStruct(xs.shape, xs.dtype),
        scratch_shapes=[pltpu.SemaphoreType.DMA(())],
        input_output_aliases={2: 0},
        compiler_params=_params("arbitrary"),
        name="moe_dispatch",
    )(slots3, h_rows, xs)


def _run_start(te_ref, i):
    return (i == 0) | (te_ref[i] != te_ref[jnp.maximum(i - 1, 0)])


def _weight_copy(w_ref, e, col, tn, buf_ref, sem_ref, part):
    return pltpu.make_async_copy(w_ref.at[e, :, pl.ds(pl.multiple_of(col, LANES), tn)],
                                 buf_ref.at[part], sem_ref.at[part])


CAST_ROWS = 512


def _cast_rows(src_ref, dst_ref):
    parts, rows, _ = src_ref.shape

    def body(c, carry):
        band = pl.ds(pl.multiple_of(c * CAST_ROWS, CAST_ROWS), CAST_ROWS)
        for p in range(parts):
            dst_ref[p, band, :] = src_ref[p, band, :].astype(BF)
        return carry

    lax.fori_loop(0, rows // CAST_ROWS, body, 0)


def _stream_weights(te_ref, nv_ref, nxt_ref, copies, cast):
    j = pl.program_id(0)
    i = pl.program_id(1)
    e = te_ref[i]

    @pl.when((i < nv_ref[0]) & _run_start(te_ref, i))
    def _():
        @pl.when((j == 0) & (i == 0))
        def _():
            for cp in copies(e, j):
                cp.start(priority=WEIGHT_DMA_PRIORITY)

        for cp in copies(e, j):
            cp.wait()
        cast()
        nxt = nxt_ref[e]
        jn = j + jnp.where(nxt <= e, 1, 0)

        @pl.when(jn < pl.num_programs(0))
        def _():
            for cp in copies(nxt, jn):
                cp.start(priority=WEIGHT_DMA_PRIORITY)


def _moe_up_kernel(te_ref, nv_ref, nxt_ref, x_ref, w_ref, bg_ref, bu_ref, o_ref, wbuf_ref, wbf_ref, sem_ref, *, e0):
    i = pl.program_id(1)
    valid = i < nv_ref[0]
    tn = o_ref.shape[1]
    half = x_ref.shape[1]

    def copies(e, j):
        return [_weight_copy(w_ref, e0 + e, part * EXPERT_DIM + j * tn, tn, wbuf_ref, sem_ref, part)
                for part in range(2)]

    def cast():
        _cast_rows(wbuf_ref, wbf_ref)

    _stream_weights(te_ref, nv_ref, nxt_ref, copies, cast)

    @pl.when(valid)
    def _():
        lo, hi = _unpack_bf16_pairs(x_ref[...])

        def proj(part):
            return (jnp.dot(lo, wbf_ref[part, :half, :], preferred_element_type=F32)
                    + jnp.dot(hi, wbf_ref[part, half:, :], preferred_element_type=F32))

        g = proj(0) + bg_ref[...]
        u = proj(1) + bu_ref[...]
        g = jnp.minimum(g, SWIGLU_LIMIT)
        u = jnp.clip(u, -SWIGLU_LIMIT, SWIGLU_LIMIT)
        o_ref[...] = (g * jax.nn.sigmoid(SWIGLU_ALPHA * g) * (u + 1.0)).astype(o_ref.dtype)

    @pl.when(jnp.logical_not(valid))
    def _():
        o_ref[...] = jnp.zeros(o_ref.shape, o_ref.dtype)


def _moe_up(tile_e, n_valid, next_e, xs, w_gu, b_gu3, layer):
    ns, half = xs.shape
    t, tn = MOE_TILE, MOE_UP_TN
    nt = ns // t
    nj = EXPERT_DIM // tn
    e0 = layer * N_EXPERTS

    def row(i, nv):
        return jnp.minimum(i, nv[0] - 1)

    grid_spec = pltpu.PrefetchScalarGridSpec(
        num_scalar_prefetch=3,
        grid=(nj, nt),
        in_specs=[pl.BlockSpec((t, half), lambda j, i, te, nv, nx: (row(i, nv), 0)),
                  pl.BlockSpec(memory_space=pl.ANY),
                  pl.BlockSpec((None, 1, tn), lambda j, i, te, nv, nx: (e0 + te[i], 0, j)),
                  pl.BlockSpec((None, 1, tn), lambda j, i, te, nv, nx: (e0 + te[i], 0, nj + j))],
        out_specs=pl.BlockSpec((t, tn), lambda j, i, te, nv, nx: (i, j)),
        scratch_shapes=[pltpu.VMEM((2, D_MODEL, tn), F32), pltpu.VMEM((2, D_MODEL, tn), BF),
                        pltpu.SemaphoreType.DMA((2,))],
    )
    return pl.pallas_call(
        functools.partial(_moe_up_kernel, e0=e0),
        grid_spec=grid_spec,
        out_shape=jax.ShapeDtypeStruct((ns, EXPERT_DIM), BF),
        compiler_params=_params("arbitrary", "arbitrary"),
        name="moe_up",
    )(tile_e, n_valid, next_e, xs, w_gu, b_gu3, b_gu3)


def _moe_down_kernel(te_ref, nv_ref, nxt_ref, a_ref, w_ref, b_ref, o_ref, wbuf_ref, wbf_ref, sem_ref, *, e0):
    i = pl.program_id(1)
    valid = i < nv_ref[0]
    tn = o_ref.shape[1]

    def copies(e, j):
        return [_weight_copy(w_ref, e0 + e, j * tn, tn, wbuf_ref, sem_ref, 0)]

    def cast():
        _cast_rows(wbuf_ref, wbf_ref)

    _stream_weights(te_ref, nv_ref, nxt_ref, copies, cast)

    @pl.when(valid)
    def _():
        o_ref[...] = jnp.dot(a_ref[...], wbf_ref[0], preferred_element_type=F32) + b_ref[...]

    @pl.when(jnp.logical_not(valid))
    def _():
        o_ref[...] = jnp.zeros(o_ref.shape, o_ref.dtype)


def _moe_down(tile_e, n_valid, next_e, act, w_down, b_down3, layer):
    ns = act.shape[0]
    t, tn = MOE_TILE, MOE_DOWN_TN
    nt = ns // t
    nj = D_MODEL // tn
    e0 = layer * N_EXPERTS

    def row(i, nv):
        return jnp.minimum(i, nv[0] - 1)

    grid_spec = pltpu.PrefetchScalarGridSpec(
        num_scalar_prefetch=3,
        grid=(nj, nt),
        in_specs=[pl.BlockSpec((t, EXPERT_DIM), lambda j, i, te, nv, nx: (row(i, nv), 0)),
                  pl.BlockSpec(memory_space=pl.ANY),
                  pl.BlockSpec((None, 1, tn), lambda j, i, te, nv, nx: (e0 + te[i], 0, j))],
        out_specs=pl.BlockSpec((t, tn), lambda j, i, te, nv, nx: (i, j)),
        scratch_shapes=[pltpu.VMEM((1, EXPERT_DIM, tn), F32), pltpu.VMEM((1, EXPERT_DIM, tn), BF),
                        pltpu.SemaphoreType.DMA((1,))],
    )
    return pl.pallas_call(
        functools.partial(_moe_down_kernel, e0=e0),
        grid_spec=grid_spec,
        out_shape=jax.ShapeDtypeStruct((ns, D_MODEL), F32),
        compiler_params=_params("arbitrary", "arbitrary"),
        name="moe_down",
    )(tile_e, n_valid, next_e, act, w_down, b_down3)


def _combine_kernel(slot_ref, next_slot_ref, y_ref, gate_ref, x_ref, g_ref, o_ref, buf_ref, sem_ref):
    tt = x_ref.shape[0]
    s = pl.program_id(0)
    cur = s % 2

    def gather(slots, half):
        def issue(r, carry):
            for k in range(TOP_K):
                _row_copy(y_ref, slots[0, r * TOP_K + k], buf_ref.at[half, k], r,
                          sem_ref.at[half]).start(priority=k % 2)
            return carry

        lax.fori_loop(0, tt, issue, 0)

    @pl.when(s == 0)
    def _():
        gather(slot_ref, 0)

    @pl.when(s + 1 < pl.num_programs(0))
    def _():
        gather(next_slot_ref, 1 - cur)

    def drain(r, carry):
        for k in range(TOP_K):
            _row_copy(y_ref, 0, buf_ref.at[cur, k], r, sem_ref.at[cur]).wait()
        return carry

    lax.fori_loop(0, tt, drain, 0)
    gates = gate_ref[...]
    acc = gates[:, 0:1] * buf_ref[cur, 0]
    for k in range(1, TOP_K):
        acc = acc + gates[:, k:k + 1] * buf_ref[cur, k]
    o_ref[...] = x_ref[...] + g_ref[...] * acc


def _combine(slots3, y_slots, gates3, x3, mod, k_gate, tt):
    b, l, _ = x3.shape
    per_tok = mod.shape[1] > 1
    lt = l // tt
    n = b * lt
    mblk = (None, tt if per_tok else 1, D_MODEL)
    slot_blk = (None, 1, tt * TOP_K)
    return pl.pallas_call(
        _combine_kernel,
        grid=(n,),
        in_specs=[pl.BlockSpec(slot_blk, lambda s: (s, 0, 0), memory_space=pltpu.SMEM),
                  pl.BlockSpec(slot_blk, lambda s: (jnp.minimum(s + 1, n - 1), 0, 0), memory_space=pltpu.SMEM),
                  pl.BlockSpec(memory_space=pl.ANY),
                  pl.BlockSpec((None, tt, LANES), lambda s: (s // lt, s % lt, 0)),
                  pl.BlockSpec((None, tt, D_MODEL), lambda s: (s // lt, s % lt, 0)),
                  pl.BlockSpec(mblk, lambda s: (s // lt, s % lt if per_tok else 0, k_gate))],
        out_specs=pl.BlockSpec((None, tt, D_MODEL), lambda s: (s // lt, s % lt, 0)),
        out_shape=jax.ShapeDtypeStruct((b, l, D_MODEL), F32),
        scratch_shapes=[pltpu.VMEM((2, TOP_K, tt, D_MODEL), F32), pltpu.SemaphoreType.DMA((2,))],
        compiler_params=_params("arbitrary"),
        name="moe_combine",
    )(slots3, slots3, y_slots, gates3, x3, mod)


def _routing_tables(top_idx, n_tokens):
    m = n_tokens * TOP_K
    t = MOE_TILE
    flat_e = top_idx.reshape(m)
    onehot = (flat_e[:, None] == jnp.arange(N_EXPERTS, dtype=I32)[None, :]).astype(I32)
    csum = jnp.cumsum(onehot, axis=0)
    pos = jnp.sum(csum * onehot, axis=1) - 1
    counts = csum[-1]
    tiles_e = (counts + t - 1) // t
    tile_end = jnp.cumsum(tiles_e)
    tile_start = tile_end - tiles_e
    slot = tile_start[flat_e] * t + pos
    n_tiles = m // t + N_EXPERTS
    n_valid = tile_end[-1]
    tile_ids = jnp.minimum(jnp.arange(n_tiles, dtype=I32), n_valid - 1)
    tile_e = jnp.minimum(jnp.sum((tile_end[None, :] <= tile_ids[:, None]).astype(I32), axis=1), N_EXPERTS - 1)
    ids = jnp.arange(N_EXPERTS, dtype=I32)
    nonempty = tiles_e > 0
    later = jnp.min(jnp.where(nonempty[None, :] & (ids[None, :] > ids[:, None]), ids[None, :], N_EXPERTS), axis=1)
    first = jnp.min(jnp.where(nonempty, ids, N_EXPERTS))
    next_e = jnp.where(later < N_EXPERTS, later, first)
    return slot.astype(I32), n_tiles * t, tile_e.astype(I32), n_valid.astype(I32).reshape(1), next_e.astype(I32)


def _mixing(xr, mod, b, l, pos, k_past, v_past, conv_state, ssm_state, weights, layer, tm):
    (norm_mix_w3, w_in, w_dt3, q_norm_w3, k_norm_w3, ssd_consts, w_out) = weights
    m = b * l
    h = _norm_mod(xr, norm_mix_w3, layer, mod, 0, 1, tm).reshape(m, D_MODEL)
    qkv = _matmul(h, w_in, layer, 0, QKV_WIDTH // 512, 512, tm)
    z = _matmul(h, w_in, layer, QKV_WIDTH // 512, SSD_WIDTH // 512, 512, tm)
    xbc = _matmul(h, w_in, layer, (QKV_WIDTH + SSD_WIDTH) // 512, CONV_DIM // 512, 512, tm)
    dtr = _matmul(h, w_dt3, layer, 0, 1, LANES, tm)

    cosf, sinf = _rope_tables(jnp.tile(pos, b))
    qn, kn = _qk_norm_rope(qkv, cosf, sinf, q_norm_w3, k_norm_w3, layer, tm)
    v = qkv[:, ATTN_WIDTH + KV_WIDTH:]

    if k_past is None:
        attn3 = _attn_prompt(qn.reshape(b, l, ATTN_WIDTH), kn.reshape(b, l, KV_WIDTH),
                             qkv.reshape(b, l, QKV_WIDTH))
        xbc3, z3, dtr3 = (a.reshape(b, l, -1) for a in (xbc, z, dtr))
        valid = SSD_CHUNK
    else:
        rows = Q_PER_KV * l
        qg = qn.reshape(b, l, KV_HEADS, Q_PER_KV, HEAD_DIM).transpose(0, 2, 3, 1, 4).reshape(b, KV_HEADS, rows, HEAD_DIM)
        padn = ((0, 0), (0, LANES - l), (0, 0))
        og = _attn_sample(qg, k_past, v_past, layer, jnp.pad(kn.reshape(b, l, KV_WIDTH), padn),
                          jnp.pad(v.reshape(b, l, KV_WIDTH), padn), l)
        attn3 = og.reshape(b, KV_HEADS, Q_PER_KV, l, HEAD_DIM).transpose(0, 3, 1, 2, 4).reshape(b, l, ATTN_WIDTH)
        attn3 = attn3.astype(BF)
        padc = ((0, 0), (0, -l % SUBLANES), (0, 0))
        xbc3, z3, dtr3 = (jnp.pad(a.reshape(b, l, -1), padc) for a in (xbc, z, dtr))
        valid = l

    conv_st8 = jnp.pad(conv_state, ((0, 0), (SUBLANES - (CONV_WIDTH - 1), 0), (0, 0)))
    ssd3, h_fin = _ssd(xbc3, z3, dtr3, conv_st8, ssm_state.reshape(b, SSD_HEADS * SSD_HEAD_DIM, SSD_STATE),
                       ssd_consts, layer, valid)
    ssd3 = ssd3[:, :l]
    x1 = _outproj(attn3.reshape(xr.shape[0], xr.shape[1], ATTN_WIDTH), ssd3.reshape(xr.shape[0], xr.shape[1], SSD_WIDTH),
                  w_out, layer, xr, mod, 2, tm, 512)
    keep = CONV_WIDTH - 1
    assert l >= keep
    new_conv = xbc.reshape(b, l, CONV_DIM)[:, l - keep:]
    return (x1, kn.reshape(b, l, KV_HEADS, HEAD_DIM), v.reshape(b, l, KV_HEADS, HEAD_DIM), new_conv,
            h_fin.reshape(b, SSD_HEADS, SSD_HEAD_DIM, SSD_STATE))


def kernel(x_prompt, x_sample, cache_k, cache_v, state_conv, state_ssm, c_prompt, c_sample, w_ada, b_ada, norm_mix_w, norm_ffn_w, w_in, q_norm_w, k_norm_w, conv_w, conv_b, dt_bias, a_log, d_skip, ssd_norm_w, w_out, router_w, router_b, w_gate_up, b_gate_up, w_down, b_down):
    depth = w_ada.shape[0]
    bp, seq, _ = x_prompt.shape
    bs, dseq, _ = x_sample.shape
    kv_win = cache_k.shape[2]
    assert kv_win == MAX_REACH and PAST_LEN >= MAX_REACH
    n_p, n_s = bp * seq, bs * dseq
    n_tok = n_p + n_s
    keep_p = min(MAX_REACH, seq)

    vec3 = lambda a: a.reshape(depth, 1, -1)
    pad_lanes = lambda a: jnp.pad(a, [(0, 0)] * (a.ndim - 1) + [(0, LANES - a.shape[-1])])
    b_ada3 = vec3(b_ada)
    norm_mix_w3, norm_ffn_w3 = vec3(norm_mix_w), vec3(norm_ffn_w)
    w_dt3 = pad_lanes(w_in[:, :, DT_COL0:])
    expand = (jnp.arange(LANES, dtype=I32)[:, None]
              == jnp.arange(SSD_WIDTH, dtype=I32)[None, :] // SSD_HEAD_DIM).astype(F32)
    ssd_consts = (conv_w, vec3(conv_b), vec3(pad_lanes(dt_bias)), vec3(pad_lanes(a_log)),
                  vec3(jnp.repeat(d_skip, SSD_HEAD_DIM, axis=-1)), vec3(ssd_norm_w), expand)
    weights = (norm_mix_w3, w_in, w_dt3, vec3(q_norm_w), vec3(k_norm_w), ssd_consts,
               w_out.reshape(depth, 2, MIX_WIDTH // 2, D_MODEL).reshape(depth * 2, MIX_WIDTH // 2, D_MODEL))
    router_wp = pad_lanes(router_w)
    router_bp = vec3(pad_lanes(router_b))
    w_gu = w_gate_up.reshape(depth * N_EXPERTS, D_MODEL, 2 * EXPERT_DIM)
    b_gu3 = b_gate_up.reshape(depth * N_EXPERTS, 1, 2 * EXPERT_DIM)
    w_dn = w_down.reshape(depth * N_EXPERTS, EXPERT_DIM, D_MODEL)
    b_dn3 = b_down.reshape(depth * N_EXPERTS, 1, D_MODEL)

    n_c = bp + bs
    c_all = jnp.concatenate([c_prompt, c_sample, jnp.zeros((-n_c % SUBLANES, D_MODEL), F32)], axis=0)
    pos_p = jnp.arange(seq, dtype=I32)
    pos_s = PAST_LEN + jnp.arange(dseq, dtype=I32)

    y_p = x_prompt
    y_s = x_sample.reshape(1, n_s, D_MODEL)
    outs = [[] for _ in range(8)]
    for layer in range(depth):
        mod = _ada(c_all, w_ada, b_ada3, layer)
        mod_p = mod[:bp].reshape(bp, 1, 6 * D_MODEL)
        mod_s = jnp.repeat(mod[bp:n_c], dseq, axis=0).reshape(1, n_s, 6 * D_MODEL)

        zero_conv = jnp.zeros((bp, CONV_WIDTH - 1, CONV_DIM), F32)
        zero_ssm = jnp.zeros((bp, SSD_HEADS, SSD_HEAD_DIM, SSD_STATE), F32)
        x1_p, kp, vp, cp, sp = _mixing(y_p, mod_p, bp, seq, pos_p, None, None, zero_conv, zero_ssm,
                                       weights, layer, 512)
        x1_s, kn, vn, cn, sn = _mixing(y_s, mod_s, bs, dseq, pos_s, cache_k, cache_v, state_conv[layer],
                                       state_ssm[layer], weights, layer, n_s)

        h2_p, idx_p, gate_p = _router(x1_p, norm_ffn_w3, layer, mod_p, 3, 4, router_wp, router_bp, 256)
        h2_s, idx_s, gate_s = _router(x1_s, norm_ffn_w3, layer, mod_s, 3, 4, router_wp, router_bp, n_s)
        top_idx = jnp.concatenate([idx_p.reshape(n_p, LANES)[:, :TOP_K], idx_s.reshape(n_s, LANES)[:, :TOP_K]], axis=0)
        slot, n_slots, tile_e, n_valid, next_e = _routing_tables(top_idx, n_tok)
        slot_p, slot_s = slot[:n_p * TOP_K], slot[n_p * TOP_K:]
        td = 256
        xs = jnp.zeros((n_slots, D_MODEL // 2), jnp.uint32)
        xs = _dispatch(slot_p.reshape(n_p // td, 1, td * TOP_K), h2_p.reshape(n_p, D_MODEL // 2), xs)
        xs = _dispatch(slot_s.reshape(1, 1, n_s * TOP_K), h2_s.reshape(n_s, D_MODEL // 2), xs)
        act = _moe_up(tile_e, n_valid, next_e, xs, w_gu, b_gu3, layer)
        y_slots = _moe_down(tile_e, n_valid, next_e, act, w_dn, b_dn3, layer)
        tt = 128
        y_p = _combine(slot_p.reshape(n_p // tt, 1, tt * TOP_K), y_slots, gate_p, x1_p, mod_p, 5, tt)
        y_s = _combine(slot_s.reshape(n_s // tt, 1, tt * TOP_K), y_slots, gate_s, x1_s, mod_s, 5, tt)

        for lst, val in zip(outs, (kp[:, seq - keep_p:], vp[:, seq - keep_p:], cp, sp, kn, vn, cn, sn)):
            lst.append(val)
    return (y_p, y_s.reshape(bs, dseq, D_MODEL)) + tuple(jnp.stack(o) for o in outs)
```

```python
import functools
import math

import jax
import jax.numpy as jnp
from jax import lax
from jax.experimental import pallas as pl
from jax.experimental.pallas import tpu as pltpu

F32 = jnp.float32
BF = jnp.bfloat16
I32 = jnp.int32

D_MODEL = 4096
PAST_LEN = 8192
HEAD_DIM = 128
ATTN_HEADS = 16
KV_HEADS = 4
Q_PER_KV = ATTN_HEADS // KV_HEADS
ATTN_WIDTH = ATTN_HEADS * HEAD_DIM
KV_WIDTH = KV_HEADS * HEAD_DIM
ROT_DIM = HEAD_DIM // 4
ROPE_THETA = 500000.0
DILATED_BRANCHES = ((128, 1), (512, 4), (2048, 16))
MAX_REACH = 2048
SSD_WIDTH = 2048
SSD_HEAD_DIM = 64
SSD_HEADS = 32
SSD_GROUPS = 8
SSD_STATE = 128
CONV_WIDTH = 4
CONV_DIM = SSD_WIDTH + 2 * SSD_GROUPS * SSD_STATE
SSD_CHUNK = 128
MIX_WIDTH = ATTN_WIDTH + SSD_WIDTH
N_EXPERTS = 32
TOP_K = 4
EXPERT_DIM = D_MODEL
SWIGLU_LIMIT = 7.0
SWIGLU_ALPHA = 1.702
EPS = 1e-6

LANES = 128
SUBLANES = 8
NEG = -1e30
QKV_WIDTH = ATTN_WIDTH + 2 * KV_WIDTH
DT_COL0 = QKV_WIDTH + SSD_WIDTH + CONV_DIM
MOE_TILE = 256
MOE_UP_TN = 1024
MOE_DOWN_TN = 1024
VMEM_LIMIT = 60 * 2 ** 20
WEIGHT_DMA_PRIORITY = 1
HI = lax.Precision.HIGHEST
NT_DIMS = (((1,), (1,)), ((), ()))
TN_DIMS = (((0,), (0,)), ((), ()))


def _params(*sem):
    return pltpu.CompilerParams(dimension_semantics=sem, vmem_limit_bytes=VMEM_LIMIT)


def _silu(x):
    return x * jax.nn.sigmoid(x)


def _ada_kernel(c_ref, w_ref, b_ref, o_ref):
    a = _silu(c_ref[...]).astype(BF)
    o_ref[...] = jnp.dot(a, w_ref[...].astype(BF), preferred_element_type=F32) + b_ref[...]


def _ada(c, w_ada, b_ada3, layer):
    mc = c.shape[0]
    n = 6 * D_MODEL
    tn = 512
    return pl.pallas_call(
        _ada_kernel,
        grid=(n // tn,),
        in_specs=[pl.BlockSpec((mc, D_MODEL), lambda j: (0, 0)),
                  pl.BlockSpec((None, D_MODEL, tn), lambda j: (layer, 0, j)),
                  pl.BlockSpec((None, 1, tn), lambda j: (layer, 0, j))],
        out_specs=pl.BlockSpec((mc, tn), lambda j: (0, j)),
        out_shape=jax.ShapeDtypeStruct((mc, n), F32),
        compiler_params=_params("arbitrary"),
        name="ada",
    )(c, w_ada, b_ada3)


def _norm_mod_kernel(x_ref, w_ref, sc_ref, sh_ref, o_ref):
    x = x_ref[...]
    y = x * lax.rsqrt(jnp.mean(x * x, axis=-1, keepdims=True) + EPS) * w_ref[...]
    o_ref[...] = (y * (1.0 + sc_ref[...]) + sh_ref[...]).astype(o_ref.dtype)


def _norm_mod(x3, norm_w3, layer, mod, k_sh, k_sc, tm):
    b, l, _ = x3.shape
    per_tok = mod.shape[1] > 1
    mblk = (None, tm if per_tok else 1, D_MODEL)

    def mspec(k):
        return pl.BlockSpec(mblk, lambda bi, i: (bi, i if per_tok else 0, k))

    return pl.pallas_call(
        _norm_mod_kernel,
        grid=(b, l // tm),
        in_specs=[pl.BlockSpec((None, tm, D_MODEL), lambda bi, i: (bi, i, 0)),
                  pl.BlockSpec((None, 1, D_MODEL), lambda bi, i: (layer, 0, 0)),
                  mspec(k_sc), mspec(k_sh)],
        out_specs=pl.BlockSpec((None, tm, D_MODEL), lambda bi, i: (bi, i, 0)),
        out_shape=jax.ShapeDtypeStruct((b, l, D_MODEL), BF),
        compiler_params=_params("arbitrary", "arbitrary"),
        name="norm_mod",
    )(x3, norm_w3, mod, mod)


def _mm_kernel(a_ref, w_ref, o_ref, wb_ref):
    @pl.when(pl.program_id(1) == 0)
    def _():
        wb_ref[...] = w_ref[...].astype(BF)

    o_ref[...] = jnp.dot(a_ref[...], wb_ref[...], preferred_element_type=F32)


def _matmul(a, w3, layer, col_blk0, n_tiles, tn, tm):
    m, k = a.shape
    return pl.pallas_call(
        _mm_kernel,
        grid=(n_tiles, m // tm),
        in_specs=[pl.BlockSpec((tm, k), lambda j, i: (i, 0)),
                  pl.BlockSpec((None, k, tn), lambda j, i: (layer, 0, col_blk0 + j))],
        out_specs=pl.BlockSpec((tm, tn), lambda j, i: (i, j)),
        out_shape=jax.ShapeDtypeStruct((m, n_tiles * tn), F32),
        scratch_shapes=[pltpu.VMEM((k, tn), BF)],
        compiler_params=_params("arbitrary", "arbitrary"),
        name="matmul",
    )(a, w3)


def _rope_tables(pos):
    half = ROT_DIM // 2
    inv_freq = jnp.power(ROPE_THETA, -jnp.arange(half, dtype=F32) * 2.0 / ROT_DIM)
    ang = pos.astype(F32)[:, None] * inv_freq[None, :]
    cos, sin = jnp.cos(ang), jnp.sin(ang)
    n = pos.shape[0]
    cosf = jnp.concatenate([cos, cos, jnp.ones((n, HEAD_DIM - ROT_DIM), F32)], axis=1)
    sinf = jnp.concatenate([-sin, sin, jnp.zeros((n, HEAD_DIM - ROT_DIM), F32)], axis=1)
    return cosf, sinf


def _norm_rope_head(xh, w, cosf, sinf, lane):
    half = ROT_DIM // 2
    y = xh * lax.rsqrt(jnp.mean(xh * xh, axis=-1, keepdims=True) + EPS) * w
    partner = jnp.where(lane < half, pltpu.roll(y, HEAD_DIM - half, 1), pltpu.roll(y, half, 1))
    return y * cosf + partner * sinf


def _qk_kernel(q_ref, k_ref, cos_ref, sin_ref, qw_ref, kw_ref, qo_ref, ko_ref):
    cosf = cos_ref[...]
    sinf = sin_ref[...]
    lane = lax.broadcasted_iota(I32, cosf.shape, 1)
    scale = HEAD_DIM ** -0.5
    for h in range(ATTN_HEADS):
        sl = slice(h * HEAD_DIM, (h + 1) * HEAD_DIM)
        qo_ref[:, sl] = _norm_rope_head(q_ref[:, sl], qw_ref[...], cosf, sinf, lane) * scale
    for h in range(KV_HEADS):
        sl = slice(h * HEAD_DIM, (h + 1) * HEAD_DIM)
        ko_ref[:, sl] = _norm_rope_head(k_ref[:, sl], kw_ref[...], cosf, sinf, lane)


def _qk_norm_rope(qkv, cosf, sinf, q_norm_w3, k_norm_w3, layer, tm):
    m = qkv.shape[0]
    return pl.pallas_call(
        _qk_kernel,
        grid=(m // tm,),
        in_specs=[pl.BlockSpec((tm, ATTN_WIDTH), lambda i: (i, 0)),
                  pl.BlockSpec((tm, KV_WIDTH), lambda i: (i, ATTN_WIDTH // KV_WIDTH)),
                  pl.BlockSpec((tm, HEAD_DIM), lambda i: (i, 0)),
                  pl.BlockSpec((tm, HEAD_DIM), lambda i: (i, 0)),
                  pl.BlockSpec((None, 1, HEAD_DIM), lambda i: (layer, 0, 0)),
                  pl.BlockSpec((None, 1, HEAD_DIM), lambda i: (layer, 0, 0))],
        out_specs=[pl.BlockSpec((tm, ATTN_WIDTH), lambda i: (i, 0)),
                   pl.BlockSpec((tm, KV_WIDTH), lambda i: (i, 0))],
        out_shape=[jax.ShapeDtypeStruct((m, ATTN_WIDTH), F32),
                   jax.ShapeDtypeStruct((m, KV_WIDTH), F32)],
        compiler_params=_params("arbitrary"),
        name="qk_norm_rope",
    )(qkv, qkv, cosf, sinf, q_norm_w3, k_norm_w3)


def _multiplicity(delta):
    ge0 = delta >= 0
    cnt = jnp.zeros(delta.shape, F32)
    for window, dil in DILATED_BRANCHES:
        hit = ge0 & (delta <= window) & ((delta & (dil - 1)) == 0)
        cnt = cnt + jnp.where(hit, 1.0, 0.0)
    return cnt


ATTN_QB = 256
ATTN_KB = 256


def _attn_prompt_kernel(q_ref, k_ref, v_ref, tab_ref, o_ref):
    c = pl.program_id(2)
    qb = q_ref.shape[0]
    kb = tab_ref.shape[1]
    per = qb // kb
    q = q_ref[...]
    qs = jnp.concatenate([q[:, h * HEAD_DIM:(h + 1) * HEAD_DIM] for h in range(Q_PER_KV)], axis=0).astype(BF)
    last_kb = per * c + per - 1
    n_it = jnp.minimum(last_kb + 1, tab_ref.shape[0])

    def body(t, carry):
        m, l, acc = carry
        start = pl.multiple_of((last_kb - t) * kb, kb)
        kblk = k_ref[pl.ds(start, kb), :].astype(BF)
        vblk = v_ref[pl.ds(start, kb), :].astype(BF)
        cnt = jnp.concatenate([tab_ref[t]] * Q_PER_KV, axis=1)
        s = lax.dot_general(kblk, qs, NT_DIMS, preferred_element_type=F32)
        sm = jnp.where(cnt > 0.0, s, NEG)
        m_new = jnp.maximum(m, jnp.max(sm, axis=0, keepdims=True))
        p = cnt * jnp.exp(sm - m_new)
        alpha = jnp.exp(m - m_new)
        l = alpha * l + jnp.sum(p, axis=0, keepdims=True)
        acc = alpha * acc + lax.dot_general(vblk, p.astype(BF), TN_DIMS, preferred_element_type=F32)
        return m_new, l, acc

    cols = Q_PER_KV * qb
    init = (jnp.full((1, cols), NEG, F32), jnp.zeros((1, cols), F32), jnp.zeros((HEAD_DIM, cols), F32))
    _, l, acc = lax.fori_loop(0, n_it, body, init)
    out_t = acc / l
    for h in range(Q_PER_KV):
        o_ref[:, h * HEAD_DIM:(h + 1) * HEAD_DIM] = out_t[:, h * qb:(h + 1) * qb].T.astype(o_ref.dtype)


def _attn_prompt(qn3, kn3, qkv3):
    b, l, _ = qn3.shape
    qb, kb = ATTN_QB, ATTN_KB
    assert l % qb == 0
    gw = Q_PER_KV * HEAD_DIM
    v_blk0 = (ATTN_WIDTH + KV_WIDTH) // HEAD_DIM
    n_tab = MAX_REACH // kb + qb // kb
    tt = jnp.arange(n_tab, dtype=I32)[:, None, None]
    jj = jnp.arange(kb, dtype=I32)[None, :, None]
    ii = jnp.arange(qb, dtype=I32)[None, None, :]
    tables = _multiplicity((tt - (qb // kb - 1)) * kb + ii - jj)
    return pl.pallas_call(
        _attn_prompt_kernel,
        grid=(b, KV_HEADS, l // qb),
        in_specs=[pl.BlockSpec((None, qb, gw), lambda bi, g, c: (bi, c, g)),
                  pl.BlockSpec((None, l, HEAD_DIM), lambda bi, g, c: (bi, 0, g)),
                  pl.BlockSpec((None, l, HEAD_DIM), lambda bi, g, c: (bi, 0, v_blk0 + g)),
                  pl.BlockSpec((n_tab, kb, qb), lambda bi, g, c: (0, 0, 0))],
        out_specs=pl.BlockSpec((None, qb, gw), lambda bi, g, c: (bi, c, g)),
        out_shape=jax.ShapeDtypeStruct((b, l, ATTN_WIDTH), BF),
        compiler_params=_params("arbitrary", "arbitrary", "arbitrary"),
        name="attn_prompt",
    )(qn3, kn3, qkv3, tables)


def _attn_sample_kernel(q_ref, ck_ref, cv_ref, kn_ref, vn_ref, o_ref, *, n_new):
    w = ck_ref.shape[0] // KV_HEADS
    pad = kn_ref.shape[0]
    rows = q_ref.shape[1]
    ri = lax.broadcasted_iota(I32, (rows, w + pad), 0) & (n_new - 1)
    cj = lax.broadcasted_iota(I32, (rows, w + pad), 1)
    cnt = _multiplicity(w + ri - cj)
    for g in range(KV_HEADS):
        sl = slice(g * HEAD_DIM, (g + 1) * HEAD_DIM)
        qg = q_ref[g].astype(BF)
        s_c = lax.dot_general(qg, ck_ref[pl.ds(g, w, stride=KV_HEADS), :].astype(BF), NT_DIMS,
                              preferred_element_type=F32)
        s_n = lax.dot_general(qg, kn_ref[:, sl].astype(BF), NT_DIMS, preferred_element_type=F32)
        s = jnp.concatenate([s_c, s_n], axis=1)
        sm = jnp.where(cnt > 0.0, s, NEG)
        m = jnp.max(sm, axis=-1, keepdims=True)
        p = cnt * jnp.exp(sm - m)
        l = jnp.sum(p, axis=-1, keepdims=True)
        pb = p.astype(BF)
        o = (jnp.dot(pb[:, :w], cv_ref[pl.ds(g, w, stride=KV_HEADS), :].astype(BF), preferred_element_type=F32)
             + jnp.dot(pb[:, w:], vn_ref[:, sl].astype(BF), preferred_element_type=F32))
        o_ref[g] = o / l


def _attn_sample(qg4, cache_k, cache_v, layer, knp3, vnp3, n_new):
    b, _, rows, _ = qg4.shape
    depth, _, w = cache_k.shape[:3]
    pad = knp3.shape[1]
    cache_k, cache_v = (a.reshape(depth, b, w * KV_HEADS, HEAD_DIM) for a in (cache_k, cache_v))
    cache_spec = pl.BlockSpec((None, None, w * KV_HEADS, HEAD_DIM), lambda i: (layer, i, 0, 0))
    return pl.pallas_call(
        functools.partial(_attn_sample_kernel, n_new=n_new),
        grid=(b,),
        in_specs=[pl.BlockSpec((None, KV_HEADS, rows, HEAD_DIM), lambda i: (i, 0, 0, 0)),
                  cache_spec, cache_spec,
                  pl.BlockSpec((None, pad, KV_WIDTH), lambda i: (i, 0, 0)),
                  pl.BlockSpec((None, pad, KV_WIDTH), lambda i: (i, 0, 0))],
        out_specs=pl.BlockSpec((None, KV_HEADS, rows, HEAD_DIM), lambda i: (i, 0, 0, 0)),
        out_shape=jax.ShapeDtypeStruct((b, KV_HEADS, rows, HEAD_DIM), F32),
        compiler_params=_params("arbitrary"),
        name="attn_sample",
    )(qg4, cache_k, cache_v, knp3, vnp3)


def _ssd_kernel(x_ref, z_ref, dtr_ref, cw_ref, cb_ref, dtb_ref, alog_ref, dsk_ref, nw_ref, cst_ref, sst_ref,
                e_ref, y_ref, hout_ref, h_ref, tail_ref, buf_ref, *, valid):
    lc = x_ref.shape[0]
    c = pl.program_id(1)
    gw = SSD_WIDTH // SSD_GROUPS
    hpg = SSD_HEADS // SSD_GROUPS
    b_col0 = SSD_WIDTH
    c_col0 = SSD_WIDTH + SSD_GROUPS * SSD_STATE

    @pl.when(c == 0)
    def _():
        h_ref[...] = sst_ref[...]
        tail_ref[...] = cst_ref[...]

    x = x_ref[...]
    buf_ref[0:SUBLANES, :] = tail_ref[...]
    buf_ref[SUBLANES:SUBLANES + lc, :] = x
    tail_ref[...] = x[lc - SUBLANES:lc, :]
    conv = cb_ref[...]
    for j in range(CONV_WIDTH):
        r0 = SUBLANES - (CONV_WIDTH - 1) + j
        conv = conv + cw_ref[j:j + 1, :] * buf_ref[r0:r0 + lc, :]
    u = _silu(conv)
    xs = u[:, :SSD_WIDTH]

    dtv = dtr_ref[...] + dtb_ref[...]
    dt = jnp.maximum(dtv, 0.0) + jnp.log1p(jnp.exp(-jnp.abs(dtv)))
    row = lax.broadcasted_iota(I32, (lc, lc), 0)
    col = lax.broadcasted_iota(I32, (lc, lc), 1)
    if valid < lc:
        dt = jnp.where(lax.broadcasted_iota(I32, dt.shape, 0) < valid, dt, 0.0)
    da = dt * (-jnp.exp(alog_ref[...]))
    causal = row >= col
    acs = jnp.dot(jnp.where(causal, 1.0, 0.0), da, precision=HI, preferred_element_type=F32)
    acs_t = acs.T
    last = acs[lc - 1:lc, :]
    expand = e_ref[...]
    dt_e = jnp.dot(dt, expand, precision=HI, preferred_element_type=F32)
    dfs_e = jnp.dot(jnp.exp(acs), expand, precision=HI, preferred_element_type=F32)
    dte_e = jnp.dot(jnp.exp(last - acs), expand, precision=HI, preferred_element_type=F32)
    xdt = xs * dt_e
    xdtw = xdt * dte_e
    lane = lax.broadcasted_iota(I32, (lc, LANES), 1)
    z = z_ref[...]

    for g in range(SSD_GROUPS):
        gs = slice(g * gw, (g + 1) * gw)
        bg = u[:, b_col0 + g * SSD_STATE:b_col0 + (g + 1) * SSD_STATE].astype(BF)
        cg = u[:, c_col0 + g * SSD_STATE:c_col0 + (g + 1) * SSD_STATE].astype(BF)
        cb = lax.dot_general(cg, bg, NT_DIMS, preferred_element_type=F32)
        hg = h_ref[g * gw:(g + 1) * gw, :]
        y_off = lax.dot_general(cg, hg.astype(BF), NT_DIMS, preferred_element_type=F32)
        pieces = []
        for pr in range(hpg // 2):
            xp = xdt[:, g * gw + pr * LANES:g * gw + (pr + 1) * LANES]
            acc = jnp.zeros((lc, LANES), F32)
            for hh in range(2):
                h = g * hpg + pr * 2 + hh
                diff = acs[:, h:h + 1] - acs_t[h:h + 1, :]
                decay = jnp.exp(jnp.where(causal, diff, -jnp.inf))
                mat = (cb * decay).astype(BF)
                keep = (lane < SSD_HEAD_DIM) if hh == 0 else (lane >= SSD_HEAD_DIM)
                acc = acc + jnp.dot(mat, jnp.where(keep, xp, 0.0).astype(BF), preferred_element_type=F32)
            pieces.append(acc)
        y_diag = jnp.concatenate(pieces, axis=1)
        yg = y_diag + y_off * dfs_e[:, gs] + xs[:, gs] * dsk_ref[:, gs]

        ds = lax.dot_general(xdtw[:, gs].astype(BF), bg, TN_DIMS, preferred_element_type=F32)
        for hh in range(hpg):
            h = g * hpg + hh
            hs = slice(hh * SSD_HEAD_DIM, (hh + 1) * SSD_HEAD_DIM)
            h_ref[g * gw + hh * SSD_HEAD_DIM:g * gw + (hh + 1) * SSD_HEAD_DIM, :] = (
                hg[hs] * jnp.exp(last[:, h:h + 1]) + ds[hs])

        yg = yg * _silu(z[:, gs])
        yg = yg * lax.rsqrt(jnp.mean(yg * yg, axis=-1, keepdims=True) + EPS)
        y_ref[:, gs] = (yg * nw_ref[:, gs]).astype(y_ref.dtype)

    @pl.when(c == pl.num_programs(1) - 1)
    def _():
        hout_ref[...] = h_ref[...]


def _ssd(xbc3, z3, dtr3, conv_st8, ssm_st3, ssd_consts, layer, valid):
    conv_w, conv_b3, dtb3, alog3, dsk3, nw3, expand = ssd_consts
    b, l, _ = xbc3.shape
    lc = min(SSD_CHUNK, l)
    vec = lambda wdt: pl.BlockSpec((None, 1, wdt), lambda bi, c: (layer, 0, 0))
    return pl.pallas_call(
        functools.partial(_ssd_kernel, valid=valid),
        grid=(b, l // lc),
        in_specs=[pl.BlockSpec((None, lc, CONV_DIM), lambda bi, c: (bi, c, 0)),
                  pl.BlockSpec((None, lc, SSD_WIDTH), lambda bi, c: (bi, c, 0)),
                  pl.BlockSpec((None, lc, LANES), lambda bi, c: (bi, c, 0)),
                  pl.BlockSpec((None, CONV_WIDTH, CONV_DIM), lambda bi, c: (layer, 0, 0)),
                  vec(CONV_DIM), vec(LANES), vec(LANES), vec(SSD_WIDTH), vec(SSD_WIDTH),
                  pl.BlockSpec((None, SUBLANES, CONV_DIM), lambda bi, c: (bi, 0, 0)),
                  pl.BlockSpec((None, SSD_HEADS * SSD_HEAD_DIM, SSD_STATE), lambda bi, c: (bi, 0, 0)),
                  pl.BlockSpec((LANES, SSD_WIDTH), lambda bi, c: (0, 0))],
        out_specs=[pl.BlockSpec((None, lc, SSD_WIDTH), lambda bi, c: (bi, c, 0)),
                   pl.BlockSpec((None, SSD_HEADS * SSD_HEAD_DIM, SSD_STATE), lambda bi, c: (bi, 0, 0))],
        out_shape=[jax.ShapeDtypeStruct((b, l, SSD_WIDTH), BF),
                   jax.ShapeDtypeStruct((b, SSD_HEADS * SSD_HEAD_DIM, SSD_STATE), F32)],
        scratch_shapes=[pltpu.VMEM((SSD_HEADS * SSD_HEAD_DIM, SSD_STATE), F32),
                        pltpu.VMEM((SUBLANES, CONV_DIM), F32),
                        pltpu.VMEM((SUBLANES + lc, CONV_DIM), F32)],
        compiler_params=_params("arbitrary", "arbitrary"),
        name="ssd",
    )(xbc3, z3, dtr3, conv_w, conv_b3, dtb3, alog3, dsk3, nw3, conv_st8, ssm_st3, expand)


def _outproj_kernel(a1_ref, a2_ref, w1_ref, w2_ref, x_ref, g_ref, o_ref, wb_ref):
    k1 = a1_ref.shape[1]

    @pl.when((pl.program_id(1) == 0) & (pl.program_id(2) == 0))
    def _():
        wb_ref[0:k1, :] = w1_ref[...].astype(BF)
        wb_ref[k1:, :] = w2_ref[...].astype(BF)

    acc = (jnp.dot(a1_ref[...], wb_ref[0:k1, :], preferred_element_type=F32)
           + jnp.dot(a2_ref[...], wb_ref[k1:, :], preferred_element_type=F32))
    o_ref[...] = x_ref[...] + g_ref[...] * acc


def _outproj(attn3, ssd3, w_out, layer, x3, mod, k_gate, tm, tn):
    b, l, k1 = attn3.shape
    k2 = ssd3.shape[2]
    per_tok = mod.shape[1] > 1
    nk = D_MODEL // tn
    mblk = (None, tm if per_tok else 1, tn)
    return pl.pallas_call(
        _outproj_kernel,
        grid=(nk, b, l // tm),
        in_specs=[pl.BlockSpec((None, tm, k1), lambda j, bi, i: (bi, i, 0)),
                  pl.BlockSpec((None, tm, k2), lambda j, bi, i: (bi, i, 0)),
                  pl.BlockSpec((None, k1, tn), lambda j, bi, i: (2 * layer, 0, j)),
                  pl.BlockSpec((None, k2, tn), lambda j, bi, i: (2 * layer + 1, 0, j)),
                  pl.BlockSpec((None, tm, tn), lambda j, bi, i: (bi, i, j)),
                  pl.BlockSpec(mblk, lambda j, bi, i: (bi, i if per_tok else 0, k_gate * nk + j))],
        out_specs=pl.BlockSpec((None, tm, tn), lambda j, bi, i: (bi, i, j)),
        out_shape=jax.ShapeDtypeStruct((b, l, D_MODEL), F32),
        scratch_shapes=[pltpu.VMEM((k1 + k2, tn), BF)],
        compiler_params=_params("arbitrary", "arbitrary", "arbitrary"),
        name="outproj",
    )(attn3, ssd3, w_out, w_out, x3, mod)


def _pack_bf16_pairs(h):
    n = h.shape[1] // 2
    bits = pltpu.bitcast(h.astype(BF).astype(F32), jnp.uint32)
    return bits[:, n:] | (bits[:, :n] >> 16)


def _unpack_bf16_pairs(w):
    lo = pltpu.bitcast(w << 16, F32).astype(BF)
    hi = pltpu.bitcast(w & jnp.uint32(0xFFFF0000), F32).astype(BF)
    return lo, hi


def _router_kernel(x_ref, w_ref, sc_ref, sh_ref, rw_ref, rb_ref, h_ref, idx_ref, gate_ref):
    x = x_ref[...]
    y = x * lax.rsqrt(jnp.mean(x * x, axis=-1, keepdims=True) + EPS) * w_ref[...]
    h = y * (1.0 + sc_ref[...]) + sh_ref[...]
    h_ref[...] = _pack_bf16_pairs(h)
    logits = jnp.dot(h, rw_ref[...], precision=HI, preferred_element_type=F32) + rb_ref[...]
    lane = lax.broadcasted_iota(I32, logits.shape, 1)
    lane_f = lane.astype(F32)
    work = jnp.where(lane < N_EXPERTS, logits, -jnp.inf)
    idx_out = jnp.zeros(logits.shape, F32)
    val_out = jnp.zeros(logits.shape, F32)
    top = None
    for k in range(TOP_K):
        m = jnp.max(work, axis=-1, keepdims=True)
        sel = jnp.min(jnp.where(work == m, lane_f, float(LANES)), axis=-1, keepdims=True)
        if k == 0:
            top = m
        idx_out = jnp.where(lane == k, sel, idx_out)
        val_out = jnp.where(lane == k, jnp.exp(m - top), val_out)
        work = jnp.where(lane_f == sel, -jnp.inf, work)
    idx_ref[...] = idx_out.astype(I32)
    gate_ref[...] = val_out / jnp.sum(val_out, axis=-1, keepdims=True)


def _router(x3, norm_w3, layer, mod, k_sh, k_sc, router_wp, router_bp, tm):
    b, l, _ = x3.shape
    per_tok = mod.shape[1] > 1
    mblk = (None, tm if per_tok else 1, D_MODEL)

    def mspec(k):
        return pl.BlockSpec(mblk, lambda bi, i: (bi, i if per_tok else 0, k))

    return pl.pallas_call(
        _router_kernel,
        grid=(b, l // tm),
        in_specs=[pl.BlockSpec((None, tm, D_MODEL), lambda bi, i: (bi, i, 0)),
                  pl.BlockSpec((None, 1, D_MODEL), lambda bi, i: (layer, 0, 0)),
                  mspec(k_sc), mspec(k_sh),
                  pl.BlockSpec((None, D_MODEL, LANES), lambda bi, i: (layer, 0, 0)),
                  pl.BlockSpec((None, 1, LANES), lambda bi, i: (layer, 0, 0))],
        out_specs=[pl.BlockSpec((None, tm, D_MODEL // 2), lambda bi, i: (bi, i, 0)),
                   pl.BlockSpec((None, tm, LANES), lambda bi, i: (bi, i, 0)),
                   pl.BlockSpec((None, tm, LANES), lambda bi, i: (bi, i, 0))],
        out_shape=[jax.ShapeDtypeStruct((b, l, D_MODEL // 2), jnp.uint32),
                   jax.ShapeDtypeStruct((b, l, LANES), I32),
                   jax.ShapeDtypeStruct((b, l, LANES), F32)],
        compiler_params=_params("arbitrary", "arbitrary"),
        name="router",
    )(x3, norm_w3, mod, mod, router_wp, router_bp)


def _row_copy(src_ref, src_row, dst_ref, dst_row, sem):
    return pltpu.make_async_copy(src_ref.at[pl.ds(src_row, 1)], dst_ref.at[pl.ds(dst_row, 1)], sem)


def _dispatch_kernel(slot_ref, h_ref, xs_in_ref, o_ref, sem):
    del xs_in_ref
    tt = h_ref.shape[0]

    def issue(r, carry):
        for k in range(TOP_K):
            _row_copy(h_ref, r, o_ref, slot_ref[0, r * TOP_K + k], sem).start(priority=k % 2)
        return carry

    lax.fori_loop(0, tt, issue, 0)

    def drain(r, carry):
        for k in range(TOP_K):
            _row_copy(h_ref, 0, o_ref, 0, sem).wait()
        return carry

    lax.fori_loop(0, tt, drain, 0)


def _dispatch(slots3, h_rows, xs):
    nsteps, _, w = slots3.shape
    tt = w // TOP_K
    return pl.pallas_call(
        _dispatch_kernel,
        grid=(nsteps,),
        in_specs=[pl.BlockSpec((None, 1, w), lambda i: (i, 0, 0), memory_space=pltpu.SMEM),
                  pl.BlockSpec((tt, h_rows.shape[1]), lambda i: (i, 0)),
                  pl.BlockSpec(memory_space=pl.ANY)],
        out_specs=pl.BlockSpec(memory_space=pl.ANY),
        out_shape=jax.ShapeDtypeStruct(xs.shape, xs.dtype),
        scratch_shapes=[pltpu.SemaphoreType.DMA(())],
        input_output_aliases={2: 0},
        compiler_params=_params("arbitrary"),
        name="moe_dispatch",
    )(slots3, h_rows, xs)


def _run_start(te_ref, i):
    return (i == 0) | (te_ref[i] != te_ref[jnp.maximum(i - 1, 0)])


def _weight_copy(w_ref, e, col, tn, buf_ref, sem_ref, part):
    return pltpu.make_async_copy(w_ref.at[e, :, pl.ds(pl.multiple_of(col, LANES), tn)],
                                 buf_ref.at[part], sem_ref.at[part])


CAST_ROWS = 512


def _cast_rows(src_ref, dst_ref):
    parts, rows, _ = src_ref.shape

    def body(c, carry):
        band = pl.ds(pl.multiple_of(c * CAST_ROWS, CAST_ROWS), CAST_ROWS)
        for p in range(parts):
            dst_ref[p, band, :] = src_ref[p, band, :].astype(BF)
        return carry

    lax.fori_loop(0, rows // CAST_ROWS, body, 0)


def _stream_weights(te_ref, nv_ref, nxt_ref, copies, cast):
    j = pl.program_id(0)
    i = pl.program_id(1)
    e = te_ref[i]

    @pl.when((i < nv_ref[0]) & _run_start(te_ref, i))
    def _():
        @pl.when((j == 0) & (i == 0))
        def _():
            for cp in copies(e, j):
                cp.start(priority=WEIGHT_DMA_PRIORITY)

        for cp in copies(e, j):
            cp.wait()
        cast()
        nxt = nxt_ref[e]
        jn = j + jnp.where(nxt <= e, 1, 0)

        @pl.when(jn < pl.num_programs(0))
        def _():
            for cp in copies(nxt, jn):
                cp.start(priority=WEIGHT_DMA_PRIORITY)


def _moe_up_kernel(te_ref, nv_ref, nxt_ref, x_ref, w_ref, bg_ref, bu_ref, o_ref, wbuf_ref, wbf_ref, sem_ref, *, e0):
    i = pl.program_id(1)
    valid = i < nv_ref[0]
    tn = o_ref.shape[1]
    half = x_ref.shape[1]

    def copies(e, j):
        return [_weight_copy(w_ref, e0 + e, part * EXPERT_DIM + j * tn, tn, wbuf_ref, sem_ref, part)
                for part in range(2)]

    def cast():
        _cast_rows(wbuf_ref, wbf_ref)

    _stream_weights(te_ref, nv_ref, nxt_ref, copies, cast)

    @pl.when(valid)
    def _():
        lo, hi = _unpack_bf16_pairs(x_ref[...])

        def proj(part):
            return (jnp.dot(lo, wbf_ref[part, :half, :], preferred_element_type=F32)
                    + jnp.dot(hi, wbf_ref[part, half:, :], preferred_element_type=F32))

        g = proj(0) + bg_ref[...]
        u = proj(1) + bu_ref[...]
        g = jnp.minimum(g, SWIGLU_LIMIT)
        u = jnp.clip(u, -SWIGLU_LIMIT, SWIGLU_LIMIT)
        o_ref[...] = (g * jax.nn.sigmoid(SWIGLU_ALPHA * g) * (u + 1.0)).astype(o_ref.dtype)

    @pl.when(jnp.logical_not(valid))
    def _():
        o_ref[...] = jnp.zeros(o_ref.shape, o_ref.dtype)


def _moe_up(tile_e, n_valid, next_e, xs, w_gu, b_gu3, layer):
    ns, half = xs.shape
    t, tn = MOE_TILE, MOE_UP_TN
    nt = ns // t
    nj = EXPERT_DIM // tn
    e0 = layer * N_EXPERTS

    def row(i, nv):
        return jnp.minimum(i, nv[0] - 1)

    grid_spec = pltpu.PrefetchScalarGridSpec(
        num_scalar_prefetch=3,
        grid=(nj, nt),
        in_specs=[pl.BlockSpec((t, half), lambda j, i, te, nv, nx: (row(i, nv), 0)),
                  pl.BlockSpec(memory_space=pl.ANY),
                  pl.BlockSpec((None, 1, tn), lambda j, i, te, nv, nx: (e0 + te[i], 0, j)),
                  pl.BlockSpec((None, 1, tn), lambda j, i, te, nv, nx: (e0 + te[i], 0, nj + j))],
        out_specs=pl.BlockSpec((t, tn), lambda j, i, te, nv, nx: (i, j)),
        scratch_shapes=[pltpu.VMEM((2, D_MODEL, tn), F32), pltpu.VMEM((2, D_MODEL, tn), BF),
                        pltpu.SemaphoreType.DMA((2,))],
    )
    return pl.pallas_call(
        functools.partial(_moe_up_kernel, e0=e0),
        grid_spec=grid_spec,
        out_shape=jax.ShapeDtypeStruct((ns, EXPERT_DIM), BF),
        compiler_params=_params("arbitrary", "arbitrary"),
        name="moe_up",
    )(tile_e, n_valid, next_e, xs, w_gu, b_gu3, b_gu3)


def _moe_down_kernel(te_ref, nv_ref, nxt_ref, a_ref, w_ref, b_ref, o_ref, wbuf_ref, wbf_ref, sem_ref, *, e0):
    i = pl.program_id(1)
    valid = i < nv_ref[0]
    tn = o_ref.shape[1]

    def copies(e, j):
        return [_weight_copy(w_ref, e0 + e, j * tn, tn, wbuf_ref, sem_ref, 0)]

    def cast():
        _cast_rows(wbuf_ref, wbf_ref)

    _stream_weights(te_ref, nv_ref, nxt_ref, copies, cast)

    @pl.when(valid)
    def _():
        o_ref[...] = jnp.dot(a_ref[...], wbf_ref[0], preferred_element_type=F32) + b_ref[...]

    @pl.when(jnp.logical_not(valid))
    def _():
        o_ref[...] = jnp.zeros(o_ref.shape, o_ref.dtype)


def _moe_down(tile_e, n_valid, next_e, act, w_down, b_down3, layer):
    ns = act.shape[0]
    t, tn = MOE_TILE, MOE_DOWN_TN
    nt = ns // t
    nj = D_MODEL // tn
    e0 = layer * N_EXPERTS

    def row(i, nv):
        return jnp.minimum(i, nv[0] - 1)

    grid_spec = pltpu.PrefetchScalarGridSpec(
        num_scalar_prefetch=3,
        grid=(nj, nt),
        in_specs=[pl.BlockSpec((t, EXPERT_DIM), lambda j, i, te, nv, nx: (row(i, nv), 0)),
                  pl.BlockSpec(memory_space=pl.ANY),
                  pl.BlockSpec((None, 1, tn), lambda j, i, te, nv, nx: (e0 + te[i], 0, j))],
        out_specs=pl.BlockSpec((t, tn), lambda j, i, te, nv, nx: (i, j)),
        scratch_shapes=[pltpu.VMEM((1, EXPERT_DIM, tn), F32), pltpu.VMEM((1, EXPERT_DIM, tn), BF),
                        pltpu.SemaphoreType.DMA((1,))],
    )
    return pl.pallas_call(
        functools.partial(_moe_down_kernel, e0=e0),
        grid_spec=grid_spec,
        out_shape=jax.ShapeDtypeStruct((ns, D_MODEL), F32),
        compiler_params=_params("arbitrary", "arbitrary"),
        name="moe_down",
    )(tile_e, n_valid, next_e, act, w_down, b_down3)


def _combine_kernel(slot_ref, next_slot_ref, y_ref, gate_ref, x_ref, g_ref, o_ref, buf_ref, sem_ref):
    tt = x_ref.shape[0]
    s = pl.program_id(0)
    cur = s % 2

    def gather(slots, half):
        def issue(r, carry):
            for k in range(TOP_K):
                _row_copy(y_ref, slots[0, r * TOP_K + k], buf_ref.at[half, k], r,
                          sem_ref.at[half]).start(priority=k % 2)
            return carry

        lax.fori_loop(0, tt, issue, 0)

    @pl.when(s == 0)
    def _():
        gather(slot_ref, 0)

    @pl.when(s + 1 < pl.num_programs(0))
    def _():
        gather(next_slot_ref, 1 - cur)

    def drain(r, carry):
        for k in range(TOP_K):
            _row_copy(y_ref, 0, buf_ref.at[cur, k], r, sem_ref.at[cur]).wait()
        return carry

    lax.fori_loop(0, tt, drain, 0)
    gates = gate_ref[...]
    acc = gates[:, 0:1] * buf_ref[cur, 0]
    for k in range(1, TOP_K):
        acc = acc + gates[:, k:k + 1] * buf_ref[cur, k]
    o_ref[...] = x_ref[...] + g_ref[...] * acc


def _combine(slots3, y_slots, gates3, x3, mod, k_gate, tt):
    b, l, _ = x3.shape
    per_tok = mod.shape[1] > 1
    lt = l // tt
    n = b * lt
    mblk = (None, tt if per_tok else 1, D_MODEL)
    slot_blk = (None, 1, tt * TOP_K)
    return pl.pallas_call(
        _combine_kernel,
        grid=(n,),
        in_specs=[pl.BlockSpec(slot_blk, lambda s: (s, 0, 0), memory_space=pltpu.SMEM),
                  pl.BlockSpec(slot_blk, lambda s: (jnp.minimum(s + 1, n - 1), 0, 0), memory_space=pltpu.SMEM),
                  pl.BlockSpec(memory_space=pl.ANY),
                  pl.BlockSpec((None, tt, LANES), lambda s: (s // lt, s % lt, 0)),
                  pl.BlockSpec((None, tt, D_MODEL), lambda s: (s // lt, s % lt, 0)),
                  pl.BlockSpec(mblk, lambda s: (s // lt, s % lt if per_tok else 0, k_gate))],
        out_specs=pl.BlockSpec((None, tt, D_MODEL), lambda s: (s // lt, s % lt, 0)),
        out_shape=jax.ShapeDtypeStruct((b, l, D_MODEL), F32),
        scratch_shapes=[pltpu.VMEM((2, TOP_K, tt, D_MODEL), F32), pltpu.SemaphoreType.DMA((2,))],
        compiler_params=_params("arbitrary"),
        name="moe_combine",
    )(slots3, slots3, y_slots, gates3, x3, mod)


def _routing_tables(top_idx, n_tokens):
    m = n_tokens * TOP_K
    t = MOE_TILE
    flat_e = top_idx.reshape(m)
    onehot = (flat_e[:, None] == jnp.arange(N_EXPERTS, dtype=I32)[None, :]).astype(I32)
    csum = jnp.cumsum(onehot, axis=0)
    pos = jnp.sum(csum * onehot, axis=1) - 1
    counts = csum[-1]
    tiles_e = (counts + t - 1) // t
    tile_end = jnp.cumsum(tiles_e)
    tile_start = tile_end - tiles_e
    slot = tile_start[flat_e] * t + pos
    n_tiles = m // t + N_EXPERTS
    n_valid = tile_end[-1]
    tile_ids = jnp.minimum(jnp.arange(n_tiles, dtype=I32), n_valid - 1)
    tile_e = jnp.minimum(jnp.sum((tile_end[None, :] <= tile_ids[:, None]).astype(I32), axis=1), N_EXPERTS - 1)
    ids = jnp.arange(N_EXPERTS, dtype=I32)
    nonempty = tiles_e > 0
    later = jnp.min(jnp.where(nonempty[None, :] & (ids[None, :] > ids[:, None]), ids[None, :], N_EXPERTS), axis=1)
    first = jnp.min(jnp.where(nonempty, ids, N_EXPERTS))
    next_e = jnp.where(later < N_EXPERTS, later, first)
    return slot.astype(I32), n_tiles * t, tile_e.astype(I32), n_valid.astype(I32).reshape(1), next_e.astype(I32)


def _mixing(xr, mod, b, l, pos, k_past, v_past, conv_state, ssm_state, weights, layer, tm):
    (norm_mix_w3, w_in, w_dt3, q_norm_w3, k_norm_w3, ssd_consts, w_out) = weights
    m = b * l
    h = _norm_mod(xr, norm_mix_w3, layer, mod, 0, 1, tm).reshape(m, D_MODEL)
    qkv = _matmul(h, w_in, layer, 0, QKV_WIDTH // 512, 512, tm)
    z = _matmul(h, w_in, layer, QKV_WIDTH // 512, SSD_WIDTH // 512, 512, tm)
    xbc = _matmul(h, w_in, layer, (QKV_WIDTH + SSD_WIDTH) // 512, CONV_DIM // 512, 512, tm)
    dtr = _matmul(h, w_dt3, layer, 0, 1, LANES, tm)

    cosf, sinf = _rope_tables(jnp.tile(pos, b))
    qn, kn = _qk_norm_rope(qkv, cosf, sinf, q_norm_w3, k_norm_w3, layer, tm)
    v = qkv[:, ATTN_WIDTH + KV_WIDTH:]

    if k_past is None:
        attn3 = _attn_prompt(qn.reshape(b, l, ATTN_WIDTH), kn.reshape(b, l, KV_WIDTH),
                             qkv.reshape(b, l, QKV_WIDTH))
        xbc3, z3, dtr3 = (a.reshape(b, l, -1) for a in (xbc, z, dtr))
        valid = SSD_CHUNK
    else:
        rows = Q_PER_KV * l
        qg = qn.reshape(b, l, KV_HEADS, Q_PER_KV, HEAD_DIM).transpose(0, 2, 3, 1, 4).reshape(b, KV_HEADS, rows, HEAD_DIM)
        padn = ((0, 0), (0, LANES - l), (0, 0))
        og = _attn_sample(qg, k_past, v_past, layer, jnp.pad(kn.reshape(b, l, KV_WIDTH), padn),
                          jnp.pad(v.reshape(b, l, KV_WIDTH), padn), l)
        attn3 = og.reshape(b, KV_HEADS, Q_PER_KV, l, HEAD_DIM).transpose(0, 3, 1, 2, 4).reshape(b, l, ATTN_WIDTH)
        attn3 = attn3.astype(BF)
        padc = ((0, 0), (0, -l % SUBLANES), (0, 0))
        xbc3, z3, dtr3 = (jnp.pad(a.reshape(b, l, -1), padc) for a in (xbc, z, dtr))
        valid = l

    conv_st8 = jnp.pad(conv_state, ((0, 0), (SUBLANES - (CONV_WIDTH - 1), 0), (0, 0)))
    ssd3, h_fin = _ssd(xbc3, z3, dtr3, conv_st8, ssm_state.reshape(b, SSD_HEADS * SSD_HEAD_DIM, SSD_STATE),
                       ssd_consts, layer, valid)
    ssd3 = ssd3[:, :l]
    x1 = _outproj(attn3.reshape(xr.shape[0], xr.shape[1], ATTN_WIDTH), ssd3.reshape(xr.shape[0], xr.shape[1], SSD_WIDTH),
                  w_out, layer, xr, mod, 2, tm, 512)
    keep = CONV_WIDTH - 1
    assert l >= keep
    new_conv = xbc.reshape(b, l, CONV_DIM)[:, l - keep:]
    return (x1, kn.reshape(b, l, KV_HEADS, HEAD_DIM), v.reshape(b, l, KV_HEADS, HEAD_DIM), new_conv,
            h_fin.reshape(b, SSD_HEADS, SSD_HEAD_DIM, SSD_STATE))


def kernel(x_prompt, x_sample, cache_k, cache_v, state_conv, state_ssm, c_prompt, c_sample, w_ada, b_ada, norm_mix_w, norm_ffn_w, w_in, q_norm_w, k_norm_w, conv_w, conv_b, dt_bias, a_log, d_skip, ssd_norm_w, w_out, router_w, router_b, w_gate_up, b_gate_up, w_down, b_down):
    depth = w_ada.shape[0]
    bp, seq, _ = x_prompt.shape
    bs, dseq, _ = x_sample.shape
    kv_win = cache_k.shape[2]
    assert kv_win == MAX_REACH and PAST_LEN >= MAX_REACH
    n_p, n_s = bp * seq, bs * dseq
    n_tok = n_p + n_s
    keep_p = min(MAX_REACH, seq)

    vec3 = lambda a: a.reshape(depth, 1, -1)
    pad_lanes = lambda a: jnp.pad(a, [(0, 0)] * (a.ndim - 1) + [(0, LANES - a.shape[-1])])
    b_ada3 = vec3(b_ada)
    norm_mix_w3, norm_ffn_w3 = vec3(norm_mix_w), vec3(norm_ffn_w)
    w_dt3 = pad_lanes(w_in[:, :, DT_COL0:])
    expand = (jnp.arange(LANES, dtype=I32)[:, None]
              == jnp.arange(SSD_WIDTH, dtype=I32)[None, :] // SSD_HEAD_DIM).astype(F32)
    ssd_consts = (conv_w, vec3(conv_b), vec3(pad_lanes(dt_bias)), vec3(pad_lanes(a_log)),
                  vec3(jnp.repeat(d_skip, SSD_HEAD_DIM, axis=-1)), vec3(ssd_norm_w), expand)
    weights = (norm_mix_w3, w_in, w_dt3, vec3(q_norm_w), vec3(k_norm_w), ssd_consts,
               w_out.reshape(depth, 2, MIX_WIDTH // 2, D_MODEL).reshape(depth * 2, MIX_WIDTH // 2, D_MODEL))
    router_wp = pad_lanes(router_w)
    router_bp = vec3(pad_lanes(router_b))
    w_gu = w_gate_up.reshape(depth * N_EXPERTS, D_MODEL, 2 * EXPERT_DIM)
    b_gu3 = b_gate_up.reshape(depth * N_EXPERTS, 1, 2 * EXPERT_DIM)
    w_dn = w_down.reshape(depth * N_EXPERTS, EXPERT_DIM, D_MODEL)
    b_dn3 = b_down.reshape(depth * N_EXPERTS, 1, D_MODEL)

    n_c = bp + bs
    c_all = jnp.concatenate([c_prompt, c_sample, jnp.zeros((-n_c % SUBLANES, D_MODEL), F32)], axis=0)
    pos_p = jnp.arange(seq, dtype=I32)
    pos_s = PAST_LEN + jnp.arange(dseq, dtype=I32)

    y_p = x_prompt
    y_s = x_sample.reshape(1, n_s, D_MODEL)
    outs = [[] for _ in range(8)]
    for layer in range(depth):
        mod = _ada(c_all, w_ada, b_ada3, layer)
        mod_p = mod[:bp].reshape(bp, 1, 6 * D_MODEL)
        mod_s = jnp.repeat(mod[bp:n_c], dseq, axis=0).reshape(1, n_s, 6 * D_MODEL)

        zero_conv = jnp.zeros((bp, CONV_WIDTH - 1, CONV_DIM), F32)
        zero_ssm = jnp.zeros((bp, SSD_HEADS, SSD_HEAD_DIM, SSD_STATE), F32)
        x1_p, kp, vp, cp, sp = _mixing(y_p, mod_p, bp, seq, pos_p, None, None, zero_conv, zero_ssm,
                                       weights, layer, 512)
        x1_s, kn, vn, cn, sn = _mixing(y_s, mod_s, bs, dseq, pos_s, cache_k, cache_v, state_conv[layer],
                                       state_ssm[layer], weights, layer, n_s)

        h2_p, idx_p, gate_p = _router(x1_p, norm_ffn_w3, layer, mod_p, 3, 4, router_wp, router_bp, 256)
        h2_s, idx_s, gate_s = _router(x1_s, norm_ffn_w3, layer, mod_s, 3, 4, router_wp, router_bp, n_s)
        top_idx = jnp.concatenate([idx_p.reshape(n_p, LANES)[:, :TOP_K], idx_s.reshape(n_s, LANES)[:, :TOP_K]], axis=0)
        slot, n_slots, tile_e, n_valid, next_e = _routing_tables(top_idx, n_tok)
        slot_p, slot_s = slot[:n_p * TOP_K], slot[n_p * TOP_K:]
        td = 256
        xs = jnp.zeros((n_slots, D_MODEL // 2), jnp.uint32)
        xs = _dispatch(slot_p.reshape(n_p // td, 1, td * TOP_K), h2_p.reshape(n_p, D_MODEL // 2), xs)
        xs = _dispatch(slot_s.reshape(1, 1, n_s * TOP_K), h2_s.reshape(n_s, D_MODEL // 2), xs)
        act = _moe_up(tile_e, n_valid, next_e, xs, w_gu, b_gu3, layer)
        y_slots = _moe_down(tile_e, n_valid, next_e, act, w_dn, b_dn3, layer)
        tt = 128
        y_p = _combine(slot_p.reshape(n_p // tt, 1, tt * TOP_K), y_slots, gate_p, x1_p, mod_p, 5, tt)
        y_s = _combine(slot_s.reshape(n_s // tt, 1, tt * TOP_K), y_slots, gate_s, x1_s, mod_s, 5, tt)

        for lst, val in zip(outs, (kp[:, seq - keep_p:], vp[:, seq - keep_p:], cp, sp, kn, vn, cn, sn)):
            lst.append(val)
    return (y_p, y_s.reshape(bs, dseq, D_MODEL)) + tuple(jnp.stack(o) for o in outs)
```

```python
import functools
import math

import jax
import jax.numpy as jnp
from jax import lax
from jax.experimental import pallas as pl
from jax.experimental.pallas import tpu as pltpu

F32 = jnp.float32
BF = jnp.bfloat16
I32 = jnp.int32

D_MODEL = 4096
PAST_LEN = 8192
HEAD_DIM = 128
ATTN_HEADS = 16
KV_HEADS = 4
Q_PER_KV = ATTN_HEADS // KV_HEADS
ATTN_WIDTH = ATTN_HEADS * HEAD_DIM
KV_WIDTH = KV_HEADS * HEAD_DIM
ROT_DIM = HEAD_DIM // 4
ROPE_THETA = 500000.0
DILATED_BRANCHES = ((128, 1), (512, 4), (2048, 16))
MAX_REACH = 2048
SSD_WIDTH = 2048
SSD_HEAD_DIM = 64
SSD_HEADS = 32
SSD_GROUPS = 8
SSD_STATE = 128
CONV_WIDTH = 4
CONV_DIM = SSD_WIDTH + 2 * SSD_GROUPS * SSD_STATE
SSD_CHUNK = 128
MIX_WIDTH = ATTN_WIDTH + SSD_WIDTH
N_EXPERTS = 32
TOP_K = 4
EXPERT_DIM = D_MODEL
SWIGLU_LIMIT = 7.0
SWIGLU_ALPHA = 1.702
EPS = 1e-6

LANES = 128
SUBLANES = 8
NEG = -1e30
QKV_WIDTH = ATTN_WIDTH + 2 * KV_WIDTH
DT_COL0 = QKV_WIDTH + SSD_WIDTH + CONV_DIM
MOE_TILE = 256
MOE_UP_TN = 1024
MOE_DOWN_TN = 2048
VMEM_LIMIT = 60 * 2 ** 20
WEIGHT_DMA_PRIORITY = 1
HI = lax.Precision.HIGHEST
NT_DIMS = (((1,), (1,)), ((), ()))
TN_DIMS = (((0,), (0,)), ((), ()))


def _params(*sem):
    return pltpu.CompilerParams(dimension_semantics=sem, vmem_limit_bytes=VMEM_LIMIT)


def _silu(x):
    return x * jax.nn.sigmoid(x)


def _ada_kernel(c_ref, w_ref, b_ref, o_ref):
    a = _silu(c_ref[...]).astype(BF)
    o_ref[...] = jnp.dot(a, w_ref[...].astype(BF), preferred_element_type=F32) + b_ref[...]


def _ada(c, w_ada, b_ada3, layer):
    mc = c.shape[0]
    n = 6 * D_MODEL
    tn = 512
    return pl.pallas_call(
        _ada_kernel,
        grid=(n // tn,),
        in_specs=[pl.BlockSpec((mc, D_MODEL), lambda j: (0, 0)),
                  pl.BlockSpec((None, D_MODEL, tn), lambda j: (layer, 0, j)),
                  pl.BlockSpec((None, 1, tn), lambda j: (layer, 0, j))],
        out_specs=pl.BlockSpec((mc, tn), lambda j: (0, j)),
        out_shape=jax.ShapeDtypeStruct((mc, n), F32),
        compiler_params=_params("arbitrary"),
        name="ada",
    )(c, w_ada, b_ada3)


def _norm_mod_kernel(x_ref, w_ref, sc_ref, sh_ref, o_ref):
    x = x_ref[...]
    y = x * lax.rsqrt(jnp.mean(x * x, axis=-1, keepdims=True) + EPS) * w_ref[...]
    o_ref[...] = (y * (1.0 + sc_ref[...]) + sh_ref[...]).astype(o_ref.dtype)


def _norm_mod(x3, norm_w3, layer, mod, k_sh, k_sc, tm):
    b, l, _ = x3.shape
    per_tok = mod.shape[1] > 1
    mblk = (None, tm if per_tok else 1, D_MODEL)

    def mspec(k):
        return pl.BlockSpec(mblk, lambda bi, i: (bi, i if per_tok else 0, k))

    return pl.pallas_call(
        _norm_mod_kernel,
        grid=(b, l // tm),
        in_specs=[pl.BlockSpec((None, tm, D_MODEL), lambda bi, i: (bi, i, 0)),
                  pl.BlockSpec((None, 1, D_MODEL), lambda bi, i: (layer, 0, 0)),
                  mspec(k_sc), mspec(k_sh)],
        out_specs=pl.BlockSpec((None, tm, D_MODEL), lambda bi, i: (bi, i, 0)),
        out_shape=jax.ShapeDtypeStruct((b, l, D_MODEL), BF),
        compiler_params=_params("arbitrary", "arbitrary"),
        name="norm_mod",
    )(x3, norm_w3, mod, mod)


def _mm_kernel(a_ref, w_ref, o_ref, wb_ref):
    @pl.when(pl.program_id(1) == 0)
    def _():
        wb_ref[...] = w_ref[...].astype(BF)

    o_ref[...] = jnp.dot(a_ref[...], wb_ref[...], preferred_element_type=F32)


def _matmul(a, w3, layer, col_blk0, n_tiles, tn, tm):
    m, k = a.shape
    return pl.pallas_call(
        _mm_kernel,
        grid=(n_tiles, m // tm),
        in_specs=[pl.BlockSpec((tm, k), lambda j, i: (i, 0)),
                  pl.BlockSpec((None, k, tn), lambda j, i: (layer, 0, col_blk0 + j))],
        out_specs=pl.BlockSpec((tm, tn), lambda j, i: (i, j)),
        out_shape=jax.ShapeDtypeStruct((m, n_tiles * tn), F32),
        scratch_shapes=[pltpu.VMEM((k, tn), BF)],
        compiler_params=_params("arbitrary", "arbitrary"),
        name="matmul",
    )(a, w3)


def _rope_tables(pos):
    half = ROT_DIM // 2
    inv_freq = jnp.power(ROPE_THETA, -jnp.arange(half, dtype=F32) * 2.0 / ROT_DIM)
    ang = pos.astype(F32)[:, None] * inv_freq[None, :]
    cos, sin = jnp.cos(ang), jnp.sin(ang)
    n = pos.shape[0]
    cosf = jnp.concatenate([cos, cos, jnp.ones((n, HEAD_DIM - ROT_DIM), F32)], axis=1)
    sinf = jnp.concatenate([-sin, sin, jnp.zeros((n, HEAD_DIM - ROT_DIM), F32)], axis=1)
    return cosf, sinf


def _norm_rope_head(xh, w, cosf, sinf, lane):
    half = ROT_DIM // 2
    y = xh * lax.rsqrt(jnp.mean(xh * xh, axis=-1, keepdims=True) + EPS) * w
    partner = jnp.where(lane < half, pltpu.roll(y, HEAD_DIM - half, 1), pltpu.roll(y, half, 1))
    return y * cosf + partner * sinf


def _qk_kernel(q_ref, k_ref, cos_ref, sin_ref, qw_ref, kw_ref, qo_ref, ko_ref):
    cosf = cos_ref[...]
    sinf = sin_ref[...]
    lane = lax.broadcasted_iota(I32, cosf.shape, 1)
    scale = HEAD_DIM ** -0.5
    for h in range(ATTN_HEADS):
        sl = slice(h * HEAD_DIM, (h + 1) * HEAD_DIM)
        qo_ref[:, sl] = _norm_rope_head(q_ref[:, sl], qw_ref[...], cosf, sinf, lane) * scale
    for h in range(KV_HEADS):
        sl = slice(h * HEAD_DIM, (h + 1) * HEAD_DIM)
        ko_ref[:, sl] = _norm_rope_head(k_ref[:, sl], kw_ref[...], cosf, sinf, lane)


def _qk_norm_rope(qkv, cosf, sinf, q_norm_w3, k_norm_w3, layer, tm):
    m = qkv.shape[0]
    return pl.pallas_call(
        _qk_kernel,
        grid=(m // tm,),
        in_specs=[pl.BlockSpec((tm, ATTN_WIDTH), lambda i: (i, 0)),
                  pl.BlockSpec((tm, KV_WIDTH), lambda i: (i, ATTN_WIDTH // KV_WIDTH)),
                  pl.BlockSpec((tm, HEAD_DIM), lambda i: (i, 0)),
                  pl.BlockSpec((tm, HEAD_DIM), lambda i: (i, 0)),
                  pl.BlockSpec((None, 1, HEAD_DIM), lambda i: (layer, 0, 0)),
                  pl.BlockSpec((None, 1, HEAD_DIM), lambda i: (layer, 0, 0))],
        out_specs=[pl.BlockSpec((tm, ATTN_WIDTH), lambda i: (i, 0)),
                   pl.BlockSpec((tm, KV_WIDTH), lambda i: (i, 0))],
        out_shape=[jax.ShapeDtypeStruct((m, ATTN_WIDTH), F32),
                   jax.ShapeDtypeStruct((m, KV_WIDTH), F32)],
        compiler_params=_params("arbitrary"),
        name="qk_norm_rope",
    )(qkv, qkv, cosf, sinf, q_norm_w3, k_norm_w3)


def _multiplicity(delta):
    ge0 = delta >= 0
    cnt = jnp.zeros(delta.shape, F32)
    for window, dil in DILATED_BRANCHES:
        hit = ge0 & (delta <= window) & ((delta & (dil - 1)) == 0)
        cnt = cnt + jnp.where(hit, 1.0, 0.0)
    return cnt


ATTN_QB = 256
ATTN_KB = 256


def _attn_prompt_kernel(q_ref, k_ref, v_ref, tab_ref, o_ref):
    c = pl.program_id(2)
    qb = q_ref.shape[0]
    kb = tab_ref.shape[1]
    per = qb // kb
    q = q_ref[...]
    qs = jnp.concatenate([q[:, h * HEAD_DIM:(h + 1) * HEAD_DIM] for h in range(Q_PER_KV)], axis=0).astype(BF)
    last_kb = per * c + per - 1
    n_it = jnp.minimum(last_kb + 1, tab_ref.shape[0])

    def body(t, carry):
        m, l, acc = carry
        start = pl.multiple_of((last_kb - t) * kb, kb)
        kblk = k_ref[pl.ds(start, kb), :].astype(BF)
        vblk = v_ref[pl.ds(start, kb), :].astype(BF)
        cnt = jnp.concatenate([tab_ref[t]] * Q_PER_KV, axis=1)
        s = lax.dot_general(kblk, qs, NT_DIMS, preferred_element_type=F32)
        sm = jnp.where(cnt > 0.0, s, NEG)
        m_new = jnp.maximum(m, jnp.max(sm, axis=0, keepdims=True))
        p = cnt * jnp.exp(sm - m_new)
        alpha = jnp.exp(m - m_new)
        l = alpha * l + jnp.sum(p, axis=0, keepdims=True)
        acc = alpha * acc + lax.dot_general(vblk, p.astype(BF), TN_DIMS, preferred_element_type=F32)
        return m_new, l, acc

    cols = Q_PER_KV * qb
    init = (jnp.full((1, cols), NEG, F32), jnp.zeros((1, cols), F32), jnp.zeros((HEAD_DIM, cols), F32))
    _, l, acc = lax.fori_loop(0, n_it, body, init)
    out_t = acc / l
    for h in range(Q_PER_KV):
        o_ref[:, h * HEAD_DIM:(h + 1) * HEAD_DIM] = out_t[:, h * qb:(h + 1) * qb].T.astype(o_ref.dtype)


def _attn_prompt(qn3, kn3, qkv3):
    b, l, _ = qn3.shape
    qb, kb = ATTN_QB, ATTN_KB
    assert l % qb == 0
    gw = Q_PER_KV * HEAD_DIM
    v_blk0 = (ATTN_WIDTH + KV_WIDTH) // HEAD_DIM
    n_tab = MAX_REACH // kb + qb // kb
    tt = jnp.arange(n_tab, dtype=I32)[:, None, None]
    jj = jnp.arange(kb, dtype=I32)[None, :, None]
    ii = jnp.arange(qb, dtype=I32)[None, None, :]
    tables = _multiplicity((tt - (qb // kb - 1)) * kb + ii - jj)
    return pl.pallas_call(
        _attn_prompt_kernel,
        grid=(b, KV_HEADS, l // qb),
        in_specs=[pl.BlockSpec((None, qb, gw), lambda bi, g, c: (bi, c, g)),
                  pl.BlockSpec((None, l, HEAD_DIM), lambda bi, g, c: (bi, 0, g)),
                  pl.BlockSpec((None, l, HEAD_DIM), lambda bi, g, c: (bi, 0, v_blk0 + g)),
                  pl.BlockSpec((n_tab, kb, qb), lambda bi, g, c: (0, 0, 0))],
        out_specs=pl.BlockSpec((None, qb, gw), lambda bi, g, c: (bi, c, g)),
        out_shape=jax.ShapeDtypeStruct((b, l, ATTN_WIDTH), BF),
        compiler_params=_params("arbitrary", "arbitrary", "arbitrary"),
        name="attn_prompt",
    )(qn3, kn3, qkv3, tables)


def _attn_sample_kernel(q_ref, ck_ref, cv_ref, kn_ref, vn_ref, o_ref, *, n_new):
    w = ck_ref.shape[0] // KV_HEADS
    pad = kn_ref.shape[0]
    rows = q_ref.shape[1]
    ri = lax.broadcasted_iota(I32, (rows, w + pad), 0) & (n_new - 1)
    cj = lax.broadcasted_iota(I32, (rows, w + pad), 1)
    cnt = _multiplicity(w + ri - cj)
    for g in range(KV_HEADS):
        sl = slice(g * HEAD_DIM, (g + 1) * HEAD_DIM)
        qg = q_ref[g].astype(BF)
        s_c = lax.dot_general(qg, ck_ref[pl.ds(g, w, stride=KV_HEADS), :].astype(BF), NT_DIMS,
                              preferred_element_type=F32)
        s_n = lax.dot_general(qg, kn_ref[:, sl].astype(BF), NT_DIMS, preferred_element_type=F32)
        s = jnp.concatenate([s_c, s_n], axis=1)
        sm = jnp.where(cnt > 0.0, s, NEG)
        m = jnp.max(sm, axis=-1, keepdims=True)
        p = cnt * jnp.exp(sm - m)
        l = jnp.sum(p, axis=-1, keepdims=True)
        pb = p.astype(BF)
        o = (jnp.dot(pb[:, :w], cv_ref[pl.ds(g, w, stride=KV_HEADS), :].astype(BF), preferred_element_type=F32)
             + jnp.dot(pb[:, w:], vn_ref[:, sl].astype(BF), preferred_element_type=F32))
        o_ref[g] = o / l


def _attn_sample(qg4, cache_k, cache_v, layer, knp3, vnp3, n_new):
    b, _, rows, _ = qg4.shape
    depth, _, w = cache_k.shape[:3]
    pad = knp3.shape[1]
    cache_k, cache_v = (a.reshape(depth, b, w * KV_HEADS, HEAD_DIM) for a in (cache_k, cache_v))
    cache_spec = pl.BlockSpec((None, None, w * KV_HEADS, HEAD_DIM), lambda i: (layer, i, 0, 0))
    return pl.pallas_call(
        functools.partial(_attn_sample_kernel, n_new=n_new),
        grid=(b,),
        in_specs=[pl.BlockSpec((None, KV_HEADS, rows, HEAD_DIM), lambda i: (i, 0, 0, 0)),
                  cache_spec, cache_spec,
                  pl.BlockSpec((None, pad, KV_WIDTH), lambda i: (i, 0, 0)),
                  pl.BlockSpec((None, pad, KV_WIDTH), lambda i: (i, 0, 0))],
        out_specs=pl.BlockSpec((None, KV_HEADS, rows, HEAD_DIM), lambda i: (i, 0, 0, 0)),
        out_shape=jax.ShapeDtypeStruct((b, KV_HEADS, rows, HEAD_DIM), F32),
        compiler_params=_params("arbitrary"),
        name="attn_sample",
    )(qg4, cache_k, cache_v, knp3, vnp3)


def _ssd_kernel(x_ref, z_ref, dtr_ref, cw_ref, cb_ref, dtb_ref, alog_ref, dsk_ref, nw_ref, cst_ref, sst_ref,
                e_ref, y_ref, hout_ref, h_ref, tail_ref, buf_ref, *, valid):
    lc = x_ref.shape[0]
    c = pl.program_id(1)
    gw = SSD_WIDTH // SSD_GROUPS
    hpg = SSD_HEADS // SSD_GROUPS
    b_col0 = SSD_WIDTH
    c_col0 = SSD_WIDTH + SSD_GROUPS * SSD_STATE

    @pl.when(c == 0)
    def _():
        h_ref[...] = sst_ref[...]
        tail_ref[...] = cst_ref[...]

    x = x_ref[...]
    buf_ref[0:SUBLANES, :] = tail_ref[...]
    buf_ref[SUBLANES:SUBLANES + lc, :] = x
    tail_ref[...] = x[lc - SUBLANES:lc, :]
    conv = cb_ref[...]
    for j in range(CONV_WIDTH):
        r0 = SUBLANES - (CONV_WIDTH - 1) + j
        conv = conv + cw_ref[j:j + 1, :] * buf_ref[r0:r0 + lc, :]
    u = _silu(conv)
    xs = u[:, :SSD_WIDTH]

    dtv = dtr_ref[...] + dtb_ref[...]
    dt = jnp.maximum(dtv, 0.0) + jnp.log1p(jnp.exp(-jnp.abs(dtv)))
    row = lax.broadcasted_iota(I32, (lc, lc), 0)
    col = lax.broadcasted_iota(I32, (lc, lc), 1)
    if valid < lc:
        dt = jnp.where(lax.broadcasted_iota(I32, dt.shape, 0) < valid, dt, 0.0)
    da = dt * (-jnp.exp(alog_ref[...]))
    causal = row >= col
    acs = jnp.dot(jnp.where(causal, 1.0, 0.0), da, precision=HI, preferred_element_type=F32)
    acs_t = acs.T
    last = acs[lc - 1:lc, :]
    expand = e_ref[...]
    dt_e = jnp.dot(dt, expand, precision=HI, preferred_element_type=F32)
    dfs_e = jnp.dot(jnp.exp(acs), expand, precision=HI, preferred_element_type=F32)
    dte_e = jnp.dot(jnp.exp(last - acs), expand, precision=HI, preferred_element_type=F32)
    xdt = xs * dt_e
    xdtw = xdt * dte_e
    lane = lax.broadcasted_iota(I32, (lc, LANES), 1)
    z = z_ref[...]

    for g in range(SSD_GROUPS):
        gs = slice(g * gw, (g + 1) * gw)
        bg = u[:, b_col0 + g * SSD_STATE:b_col0 + (g + 1) * SSD_STATE].astype(BF)
        cg = u[:, c_col0 + g * SSD_STATE:c_col0 + (g + 1) * SSD_STATE].astype(BF)
        cb = lax.dot_general(cg, bg, NT_DIMS, preferred_element_type=F32)
        hg = h_ref[g * gw:(g + 1) * gw, :]
        y_off = lax.dot_general(cg, hg.astype(BF), NT_DIMS, preferred_element_type=F32)
        pieces = []
        for pr in range(hpg // 2):
            xp = xdt[:, g * gw + pr * LANES:g * gw + (pr + 1) * LANES]
            acc = jnp.zeros((lc, LANES), F32)
            for hh in range(2):
                h = g * hpg + pr * 2 + hh
                diff = acs[:, h:h + 1] - acs_t[h:h + 1, :]
                decay = jnp.exp(jnp.where(causal, diff, -jnp.inf))
                mat = (cb * decay).astype(BF)
                keep = (lane < SSD_HEAD_DIM) if hh == 0 else (lane >= SSD_HEAD_DIM)
                acc = acc + jnp.dot(mat, jnp.where(keep, xp, 0.0).astype(BF), preferred_element_type=F32)
            pieces.append(acc)
        y_diag = jnp.concatenate(pieces, axis=1)
        yg = y_diag + y_off * dfs_e[:, gs] + xs[:, gs] * dsk_ref[:, gs]

        ds = lax.dot_general(xdtw[:, gs].astype(BF), bg, TN_DIMS, preferred_element_type=F32)
        for hh in range(hpg):
            h = g * hpg + hh
            hs = slice(hh * SSD_HEAD_DIM, (hh + 1) * SSD_HEAD_DIM)
            h_ref[g * gw + hh * SSD_HEAD_DIM:g * gw + (hh + 1) * SSD_HEAD_DIM, :] = (
                hg[hs] * jnp.exp(last[:, h:h + 1]) + ds[hs])

        yg = yg * _silu(z[:, gs])
        yg = yg * lax.rsqrt(jnp.mean(yg * yg, axis=-1, keepdims=True) + EPS)
        y_ref[:, gs] = (yg * nw_ref[:, gs]).astype(y_ref.dtype)

    @pl.when(c == pl.num_programs(1) - 1)
    def _():
        hout_ref[...] = h_ref[...]


def _ssd(xbc3, z3, dtr3, conv_st8, ssm_st3, ssd_consts, layer, valid):
    conv_w, conv_b3, dtb3, alog3, dsk3, nw3, expand = ssd_consts
    b, l, _ = xbc3.shape
    lc = min(SSD_CHUNK, l)
    vec = lambda wdt: pl.BlockSpec((None, 1, wdt), lambda bi, c: (layer, 0, 0))
    return pl.pallas_call(
        functools.partial(_ssd_kernel, valid=valid),
        grid=(b, l // lc),
        in_specs=[pl.BlockSpec((None, lc, CONV_DIM), lambda bi, c: (bi, c, 0)),
                  pl.BlockSpec((None, lc, SSD_WIDTH), lambda bi, c: (bi, c, 0)),
                  pl.BlockSpec((None, lc, LANES), lambda bi, c: (bi, c, 0)),
                  pl.BlockSpec((None, CONV_WIDTH, CONV_DIM), lambda bi, c: (layer, 0, 0)),
                  vec(CONV_DIM), vec(LANES), vec(LANES), vec(SSD_WIDTH), vec(SSD_WIDTH),
                  pl.BlockSpec((None, SUBLANES, CONV_DIM), lambda bi, c: (bi, 0, 0)),
                  pl.BlockSpec((None, SSD_HEADS * SSD_HEAD_DIM, SSD_STATE), lambda bi, c: (bi, 0, 0)),
                  pl.BlockSpec((LANES, SSD_WIDTH), lambda bi, c: (0, 0))],
        out_specs=[pl.BlockSpec((None, lc, SSD_WIDTH), lambda bi, c: (bi, c, 0)),
                   pl.BlockSpec((None, SSD_HEADS * SSD_HEAD_DIM, SSD_STATE), lambda bi, c: (bi, 0, 0))],
        out_shape=[jax.ShapeDtypeStruct((b, l, SSD_WIDTH), BF),
                   jax.ShapeDtypeStruct((b, SSD_HEADS * SSD_HEAD_DIM, SSD_STATE), F32)],
        scratch_shapes=[pltpu.VMEM((SSD_HEADS * SSD_HEAD_DIM, SSD_STATE), F32),
                        pltpu.VMEM((SUBLANES, CONV_DIM), F32),
                        pltpu.VMEM((SUBLANES + lc, CONV_DIM), F32)],
        compiler_params=_params("arbitrary", "arbitrary"),
        name="ssd",
    )(xbc3, z3, dtr3, conv_w, conv_b3, dtb3, alog3, dsk3, nw3, conv_st8, ssm_st3, expand)


def _outproj_kernel(a1_ref, a2_ref, w1_ref, w2_ref, x_ref, g_ref, o_ref, wb_ref):
    k1 = a1_ref.shape[1]

    @pl.when((pl.program_id(1) == 0) & (pl.program_id(2) == 0))
    def _():
        wb_ref[0:k1, :] = w1_ref[...].astype(BF)
        wb_ref[k1:, :] = w2_ref[...].astype(BF)

    acc = (jnp.dot(a1_ref[...], wb_ref[0:k1, :], preferred_element_type=F32)
           + jnp.dot(a2_ref[...], wb_ref[k1:, :], preferred_element_type=F32))
    o_ref[...] = x_ref[...] + g_ref[...] * acc


def _outproj(attn3, ssd3, w_out, layer, x3, mod, k_gate, tm, tn):
    b, l, k1 = attn3.shape
    k2 = ssd3.shape[2]
    per_tok = mod.shape[1] > 1
    nk = D_MODEL // tn
    mblk = (None, tm if per_tok else 1, tn)
    return pl.pallas_call(
        _outproj_kernel,
        grid=(nk, b, l // tm),
        in_specs=[pl.BlockSpec((None, tm, k1), lambda j, bi, i: (bi, i, 0)),
                  pl.BlockSpec((None, tm, k2), lambda j, bi, i: (bi, i, 0)),
                  pl.BlockSpec((None, k1, tn), lambda j, bi, i: (2 * layer, 0, j)),
                  pl.BlockSpec((None, k2, tn), lambda j, bi, i: (2 * layer + 1, 0, j)),
                  pl.BlockSpec((None, tm, tn), lambda j, bi, i: (bi, i, j)),
                  pl.BlockSpec(mblk, lambda j, bi, i: (bi, i if per_tok else 0, k_gate * nk + j))],
        out_specs=pl.BlockSpec((None, tm, tn), lambda j, bi, i: (bi, i, j)),
        out_shape=jax.ShapeDtypeStruct((b, l, D_MODEL), F32),
        scratch_shapes=[pltpu.VMEM((k1 + k2, tn), BF)],
        compiler_params=_params("arbitrary", "arbitrary", "arbitrary"),
        name="outproj",
    )(attn3, ssd3, w_out, w_out, x3, mod)


def _pack_bf16_pairs(h):
    n = h.shape[1] // 2
    bits = pltpu.bitcast(h.astype(BF).astype(F32), jnp.uint32)
    return bits[:, n:] | (bits[:, :n] >> 16)


def _unpack_bf16_pairs(w):
    lo = pltpu.bitcast(w << 16, F32).astype(BF)
    hi = pltpu.bitcast(w & jnp.uint32(0xFFFF0000), F32).astype(BF)
    return lo, hi


def _router_kernel(x_ref, w_ref, sc_ref, sh_ref, rw_ref, rb_ref, h_ref, idx_ref, gate_ref):
    x = x_ref[...]
    y = x * lax.rsqrt(jnp.mean(x * x, axis=-1, keepdims=True) + EPS) * w_ref[...]
    h = y * (1.0 + sc_ref[...]) + sh_ref[...]
    h_ref[...] = _pack_bf16_pairs(h)
    logits = jnp.dot(h, rw_ref[...], precision=HI, preferred_element_type=F32) + rb_ref[...]
    lane = lax.broadcasted_iota(I32, logits.shape, 1)
    lane_f = lane.astype(F32)
    work = jnp.where(lane < N_EXPERTS, logits, -jnp.inf)
    idx_out = jnp.zeros(logits.shape, F32)
    val_out = jnp.zeros(logits.shape, F32)
    top = None
    for k in range(TOP_K):
        m = jnp.max(work, axis=-1, keepdims=True)
        sel = jnp.min(jnp.where(work == m, lane_f, float(LANES)), axis=-1, keepdims=True)
        if k == 0:
            top = m
        idx_out = jnp.where(lane == k, sel, idx_out)
        val_out = jnp.where(lane == k, jnp.exp(m - top), val_out)
        work = jnp.where(lane_f == sel, -jnp.inf, work)
    idx_ref[...] = idx_out.astype(I32)
    gate_ref[...] = val_out / jnp.sum(val_out, axis=-1, keepdims=True)


def _router(x3, norm_w3, layer, mod, k_sh, k_sc, router_wp, router_bp, tm):
    b, l, _ = x3.shape
    per_tok = mod.shape[1] > 1
    mblk = (None, tm if per_tok else 1, D_MODEL)

    def mspec(k):
        return pl.BlockSpec(mblk, lambda bi, i: (bi, i if per_tok else 0, k))

    return pl.pallas_call(
        _router_kernel,
        grid=(b, l // tm),
        in_specs=[pl.BlockSpec((None, tm, D_MODEL), lambda bi, i: (bi, i, 0)),
                  pl.BlockSpec((None, 1, D_MODEL), lambda bi, i: (layer, 0, 0)),
                  mspec(k_sc), mspec(k_sh),
                  pl.BlockSpec((None, D_MODEL, LANES), lambda bi, i: (layer, 0, 0)),
                  pl.BlockSpec((None, 1, LANES), lambda bi, i: (layer, 0, 0))],
        out_specs=[pl.BlockSpec((None, tm, D_MODEL // 2), lambda bi, i: (bi, i, 0)),
                   pl.BlockSpec((None, tm, LANES), lambda bi, i: (bi, i, 0)),
                   pl.BlockSpec((None, tm, LANES), lambda bi, i: (bi, i, 0))],
        out_shape=[jax.ShapeDtypeStruct((b, l, D_MODEL // 2), jnp.uint32),
                   jax.ShapeDtypeStruct((b, l, LANES), I32),
                   jax.ShapeDtypeStruct((b, l, LANES), F32)],
        compiler_params=_params("arbitrary", "arbitrary"),
        name="router",
    )(x3, norm_w3, mod, mod, router_wp, router_bp)


def _row_copy(src_ref, src_row, dst_ref, dst_row, sem):
    return pltpu.make_async_copy(src_ref.at[pl.ds(src_row, 1)], dst_ref.at[pl.ds(dst_row, 1)], sem)


def _dispatch_kernel(slot_ref, h_ref, xs_in_ref, o_ref, sem):
    del xs_in_ref
    tt = h_ref.shape[0]

    def issue(r, carry):
        for k in range(TOP_K):
            _row_copy(h_ref, r, o_ref, slot_ref[0, r * TOP_K + k], sem).start(priority=k % 2)
        return carry

    lax.fori_loop(0, tt, issue, 0)

    def drain(r, carry):
        for k in range(TOP_K):
            _row_copy(h_ref, 0, o_ref, 0, sem).wait()
        return carry

    lax.fori_loop(0, tt, drain, 0)


def _dispatch(slots3, h_rows, xs):
    nsteps, _, w = slots3.shape
    tt = w // TOP_K
    return pl.pallas_call(
        _dispatch_kernel,
        grid=(nsteps,),
        in_specs=[pl.BlockSpec((None, 1, w), lambda i: (i, 0, 0), memory_space=pltpu.SMEM),
                  pl.BlockSpec((tt, h_rows.shape[1]), lambda i: (i, 0)),
                  pl.BlockSpec(memory_space=pl.ANY)],
        out_specs=pl.BlockSpec(memory_space=pl.ANY),
        out_shape=jax.ShapeDtypeStruct(xs.shape, xs.dtype),
        scratch_shapes=[pltpu.SemaphoreType.DMA(())],
        input_output_aliases={2: 0},
        compiler_params=_params("arbitrary"),
        name="moe_dispatch",
    )(slots3, h_rows, xs)


def _run_start(te_ref, i):
    return (i == 0) | (te_ref[i] != te_ref[jnp.maximum(i - 1, 0)])


def _weight_copy(w_ref, e, col, tn, buf_ref, sem_ref, part):
    return pltpu.make_async_copy(w_ref.at[e, :, pl.ds(pl.multiple_of(col, LANES), tn)],
                                 buf_ref.at[part], sem_ref.at[part])


CAST_ROWS = 512


def _cast_rows(src_ref, dst_ref):
    parts, rows, _ = src_ref.shape

    def body(c, carry):
        band = pl.ds(pl.multiple_of(c * CAST_ROWS, CAST_ROWS), CAST_ROWS)
        for p in range(parts):
            dst_ref[p, band, :] = src_ref[p, band, :].astype(BF)
        return carry

    lax.fori_loop(0, rows // CAST_ROWS, body, 0)


def _stream_weights(te_ref, nv_ref, nxt_ref, copies, cast):
    j = pl.program_id(0)
    i = pl.program_id(1)
    e = te_ref[i]

    @pl.when((i < nv_ref[0]) & _run_start(te_ref, i))
    def _():
        @pl.when((j == 0) & (i == 0))
        def _():
            for cp in copies(e, j):
                cp.start(priority=WEIGHT_DMA_PRIORITY)

        for cp in copies(e, j):
            cp.wait()
        cast()
        nxt = nxt_ref[e]
        jn = j + jnp.where(nxt <= e, 1, 0)

        @pl.when(jn < pl.num_programs(0))
        def _():
            for cp in copies(nxt, jn):
                cp.start(priority=WEIGHT_DMA_PRIORITY)


def _moe_up_kernel(te_ref, nv_ref, nxt_ref, x_ref, w_ref, bg_ref, bu_ref, o_ref, wbuf_ref, wbf_ref, sem_ref, *, e0):
    i = pl.program_id(1)
    valid = i < nv_ref[0]
    tn = o_ref.shape[1]
    half = x_ref.shape[1]

    def copies(e, j):
        return [_weight_copy(w_ref, e0 + e, part * EXPERT_DIM + j * tn, tn, wbuf_ref, sem_ref, part)
                for part in range(2)]

    def cast():
        _cast_rows(wbuf_ref, wbf_ref)

    _stream_weights(te_ref, nv_ref, nxt_ref, copies, cast)

    @pl.when(valid)
    def _():
        lo, hi = _unpack_bf16_pairs(x_ref[...])

        def proj(part):
            return (jnp.dot(lo, wbf_ref[part, :half, :], preferred_element_type=F32)
                    + jnp.dot(hi, wbf_ref[part, half:, :], preferred_element_type=F32))

        g = proj(0) + bg_ref[...]
        u = proj(1) + bu_ref[...]
        g = jnp.minimum(g, SWIGLU_LIMIT)
        u = jnp.clip(u, -SWIGLU_LIMIT, SWIGLU_LIMIT)
        o_ref[...] = (g * jax.nn.sigmoid(SWIGLU_ALPHA * g) * (u + 1.0)).astype(o_ref.dtype)

    @pl.when(jnp.logical_not(valid))
    def _():
        o_ref[...] = jnp.zeros(o_ref.shape, o_ref.dtype)


def _moe_up(tile_e, n_valid, next_e, xs, w_gu, b_gu3, layer):
    ns, half = xs.shape
    t, tn = MOE_TILE, MOE_UP_TN
    nt = ns // t
    nj = EXPERT_DIM // tn
    e0 = layer * N_EXPERTS

    def row(i, nv):
        return jnp.minimum(i, nv[0] - 1)

    grid_spec = pltpu.PrefetchScalarGridSpec(
        num_scalar_prefetch=3,
        grid=(nj, nt),
        in_specs=[pl.BlockSpec((t, half), lambda j, i, te, nv, nx: (row(i, nv), 0)),
                  pl.BlockSpec(memory_space=pl.ANY),
                  pl.BlockSpec((None, 1, tn), lambda j, i, te, nv, nx: (e0 + te[i], 0, j)),
                  pl.BlockSpec((None, 1, tn), lambda j, i, te, nv, nx: (e0 + te[i], 0, nj + j))],
        out_specs=pl.BlockSpec((t, tn), lambda j, i, te, nv, nx: (i, j)),
        scratch_shapes=[pltpu.VMEM((2, D_MODEL, tn), F32), pltpu.VMEM((2, D_MODEL, tn), BF),
                        pltpu.SemaphoreType.DMA((2,))],
    )
    return pl.pallas_call(
        functools.partial(_moe_up_kernel, e0=e0),
        grid_spec=grid_spec,
        out_shape=jax.ShapeDtypeStruct((ns, EXPERT_DIM), BF),
        compiler_params=_params("arbitrary", "arbitrary"),
        name="moe_up",
    )(tile_e, n_valid, next_e, xs, w_gu, b_gu3, b_gu3)


def _moe_down_kernel(te_ref, nv_ref, nxt_ref, a_ref, w_ref, b_ref, o_ref, wbuf_ref, wbf_ref, sem_ref, *, e0):
    i = pl.program_id(1)
    valid = i < nv_ref[0]
    tn = o_ref.shape[1]

    def copies(e, j):
        return [_weight_copy(w_ref, e0 + e, j * tn, tn, wbuf_ref, sem_ref, 0)]

    def cast():
        _cast_rows(wbuf_ref, wbf_ref)

    _stream_weights(te_ref, nv_ref, nxt_ref, copies, cast)

    @pl.when(valid)
    def _():
        o_ref[...] = jnp.dot(a_ref[...], wbf_ref[0], preferred_element_type=F32) + b_ref[...]

    @pl.when(jnp.logical_not(valid))
    def _():
        o_ref[...] = jnp.zeros(o_ref.shape, o_ref.dtype)


def _moe_down(tile_e, n_valid, next_e, act, w_down, b_down3, layer):
    ns = act.shape[0]
    t, tn = MOE_TILE, MOE_DOWN_TN
    nt = ns // t
    nj = D_MODEL // tn
    e0 = layer * N_EXPERTS

    def row(i, nv):
        return jnp.minimum(i, nv[0] - 1)

    grid_spec = pltpu.PrefetchScalarGridSpec(
        num_scalar_prefetch=3,
        grid=(nj, nt),
        in_specs=[pl.BlockSpec((t, EXPERT_DIM), lambda j, i, te, nv, nx: (row(i, nv), 0)),
                  pl.BlockSpec(memory_space=pl.ANY),
                  pl.BlockSpec((None, 1, tn), lambda j, i, te, nv, nx: (e0 + te[i], 0, j))],
        out_specs=pl.BlockSpec((t, tn), lambda j, i, te, nv, nx: (i, j)),
        scratch_shapes=[pltpu.VMEM((1, EXPERT_DIM, tn), F32), pltpu.VMEM((1, EXPERT_DIM, tn), BF),
                        pltpu.SemaphoreType.DMA((1,))],
    )
    return pl.pallas_call(
        functools.partial(_moe_down_kernel, e0=e0),
        grid_spec=grid_spec,
        out_shape=jax.ShapeDtypeStruct((ns, D_MODEL), F32),
        compiler_params=_params("arbitrary", "arbitrary"),
        name="moe_down",
    )(tile_e, n_valid, next_e, act, w_down, b_down3)


def _combine_kernel(slot_ref, next_slot_ref, y_ref, gate_ref, x_ref, g_ref, o_ref, buf_ref, sem_ref):
    tt = x_ref.shape[0]
    s = pl.program_id(0)
    cur = s % 2

    def gather(slots, half):
        def issue(r, carry):
            for k in range(TOP_K):
                _row_copy(y_ref, slots[0, r * TOP_K + k], buf_ref.at[half, k], r,
                          sem_ref.at[half]).start(priority=k % 2)
            return carry

        lax.fori_loop(0, tt, issue, 0)

    @pl.when(s == 0)
    def _():
        gather(slot_ref, 0)

    @pl.when(s + 1 < pl.num_programs(0))
    def _():
        gather(next_slot_ref, 1 - cur)

    def drain(r, carry):
        for k in range(TOP_K):
            _row_copy(y_ref, 0, buf_ref.at[cur, k], r, sem_ref.at[cur]).wait()
        return carry

    lax.fori_loop(0, tt, drain, 0)
    gates = gate_ref[...]
    acc = gates[:, 0:1] * buf_ref[cur, 0]
    for k in range(1, TOP_K):
        acc = acc + gates[:, k:k + 1] * buf_ref[cur, k]
    o_ref[...] = x_ref[...] + g_ref[...] * acc


def _combine(slots3, y_slots, gates3, x3, mod, k_gate, tt):
    b, l, _ = x3.shape
    per_tok = mod.shape[1] > 1
    lt = l // tt
    n = b * lt
    mblk = (None, tt if per_tok else 1, D_MODEL)
    slot_blk = (None, 1, tt * TOP_K)
    return pl.pallas_call(
        _combine_kernel,
        grid=(n,),
        in_specs=[pl.BlockSpec(slot_blk, lambda s: (s, 0, 0), memory_space=pltpu.SMEM),
                  pl.BlockSpec(slot_blk, lambda s: (jnp.minimum(s + 1, n - 1), 0, 0), memory_space=pltpu.SMEM),
                  pl.BlockSpec(memory_space=pl.ANY),
                  pl.BlockSpec((None, tt, LANES), lambda s: (s // lt, s % lt, 0)),
                  pl.BlockSpec((None, tt, D_MODEL), lambda s: (s // lt, s % lt, 0)),
                  pl.BlockSpec(mblk, lambda s: (s // lt, s % lt if per_tok else 0, k_gate))],
        out_specs=pl.BlockSpec((None, tt, D_MODEL), lambda s: (s // lt, s % lt, 0)),
        out_shape=jax.ShapeDtypeStruct((b, l, D_MODEL), F32),
        scratch_shapes=[pltpu.VMEM((2, TOP_K, tt, D_MODEL), F32), pltpu.SemaphoreType.DMA((2,))],
        compiler_params=_params("arbitrary"),
        name="moe_combine",
    )(slots3, slots3, y_slots, gates3, x3, mod)


def _routing_tables(top_idx, n_tokens):
    m = n_tokens * TOP_K
    t = MOE_TILE
    flat_e = top_idx.reshape(m)
    onehot = (flat_e[:, None] == jnp.arange(N_EXPERTS, dtype=I32)[None, :]).astype(I32)
    csum = jnp.cumsum(onehot, axis=0)
    pos = jnp.sum(csum * onehot, axis=1) - 1
    counts = csum[-1]
    tiles_e = (counts + t - 1) // t
    tile_end = jnp.cumsum(tiles_e)
    tile_start = tile_end - tiles_e
    slot = tile_start[flat_e] * t + pos
    n_tiles = m // t + N_EXPERTS
    n_valid = tile_end[-1]
    tile_ids = jnp.minimum(jnp.arange(n_tiles, dtype=I32), n_valid - 1)
    tile_e = jnp.minimum(jnp.sum((tile_end[None, :] <= tile_ids[:, None]).astype(I32), axis=1), N_EXPERTS - 1)
    ids = jnp.arange(N_EXPERTS, dtype=I32)
    nonempty = tiles_e > 0
    later = jnp.min(jnp.where(nonempty[None, :] & (ids[None, :] > ids[:, None]), ids[None, :], N_EXPERTS), axis=1)
    first = jnp.min(jnp.where(nonempty, ids, N_EXPERTS))
    next_e = jnp.where(later < N_EXPERTS, later, first)
    return slot.astype(I32), n_tiles * t, tile_e.astype(I32), n_valid.astype(I32).reshape(1), next_e.astype(I32)


def _mixing(xr, mod, b, l, pos, k_past, v_past, conv_state, ssm_state, weights, layer, tm):
    (norm_mix_w3, w_in, w_dt3, q_norm_w3, k_norm_w3, ssd_consts, w_out) = weights
    m = b * l
    h = _norm_mod(xr, norm_mix_w3, layer, mod, 0, 1, tm).reshape(m, D_MODEL)
    qkv = _matmul(h, w_in, layer, 0, QKV_WIDTH // 512, 512, tm)
    z = _matmul(h, w_in, layer, QKV_WIDTH // 512, SSD_WIDTH // 512, 512, tm)
    xbc = _matmul(h, w_in, layer, (QKV_WIDTH + SSD_WIDTH) // 512, CONV_DIM // 512, 512, tm)
    dtr = _matmul(h, w_dt3, layer, 0, 1, LANES, tm)

    cosf, sinf = _rope_tables(jnp.tile(pos, b))
    qn, kn = _qk_norm_rope(qkv, cosf, sinf, q_norm_w3, k_norm_w3, layer, tm)
    v = qkv[:, ATTN_WIDTH + KV_WIDTH:]

    if k_past is None:
        attn3 = _attn_prompt(qn.reshape(b, l, ATTN_WIDTH), kn.reshape(b, l, KV_WIDTH),
                             qkv.reshape(b, l, QKV_WIDTH))
        xbc3, z3, dtr3 = (a.reshape(b, l, -1) for a in (xbc, z, dtr))
        valid = SSD_CHUNK
    else:
        rows = Q_PER_KV * l
        qg = qn.reshape(b, l, KV_HEADS, Q_PER_KV, HEAD_DIM).transpose(0, 2, 3, 1, 4).reshape(b, KV_HEADS, rows, HEAD_DIM)
        padn = ((0, 0), (0, LANES - l), (0, 0))
        og = _attn_sample(qg, k_past, v_past, layer, jnp.pad(kn.reshape(b, l, KV_WIDTH), padn),
                          jnp.pad(v.reshape(b, l, KV_WIDTH), padn), l)
        attn3 = og.reshape(b, KV_HEADS, Q_PER_KV, l, HEAD_DIM).transpose(0, 3, 1, 2, 4).reshape(b, l, ATTN_WIDTH)
        attn3 = attn3.astype(BF)
        padc = ((0, 0), (0, -l % SUBLANES), (0, 0))
        xbc3, z3, dtr3 = (jnp.pad(a.reshape(b, l, -1), padc) for a in (xbc, z, dtr))
        valid = l

    conv_st8 = jnp.pad(conv_state, ((0, 0), (SUBLANES - (CONV_WIDTH - 1), 0), (0, 0)))
    ssd3, h_fin = _ssd(xbc3, z3, dtr3, conv_st8, ssm_state.reshape(b, SSD_HEADS * SSD_HEAD_DIM, SSD_STATE),
                       ssd_consts, layer, valid)
    ssd3 = ssd3[:, :l]
    x1 = _outproj(attn3.reshape(xr.shape[0], xr.shape[1], ATTN_WIDTH), ssd3.reshape(xr.shape[0], xr.shape[1], SSD_WIDTH),
                  w_out, layer, xr, mod, 2, tm, 512)
    keep = CONV_WIDTH - 1
    assert l >= keep
    new_conv = xbc.reshape(b, l, CONV_DIM)[:, l - keep:]
    return (x1, kn.reshape(b, l, KV_HEADS, HEAD_DIM), v.reshape(b, l, KV_HEADS, HEAD_DIM), new_conv,
            h_fin.reshape(b, SSD_HEADS, SSD_HEAD_DIM, SSD_STATE))


def kernel(x_prompt, x_sample, cache_k, cache_v, state_conv, state_ssm, c_prompt, c_sample, w_ada, b_ada, norm_mix_w, norm_ffn_w, w_in, q_norm_w, k_norm_w, conv_w, conv_b, dt_bias, a_log, d_skip, ssd_norm_w, w_out, router_w, router_b, w_gate_up, b_gate_up, w_down, b_down):
    depth = w_ada.shape[0]
    bp, seq, _ = x_prompt.shape
    bs, dseq, _ = x_sample.shape
    kv_win = cache_k.shape[2]
    assert kv_win == MAX_REACH and PAST_LEN >= MAX_REACH
    n_p, n_s = bp * seq, bs * dseq
    n_tok = n_p + n_s
    keep_p = min(MAX_REACH, seq)

    vec3 = lambda a: a.reshape(depth, 1, -1)
    pad_lanes = lambda a: jnp.pad(a, [(0, 0)] * (a.ndim - 1) + [(0, LANES - a.shape[-1])])
    b_ada3 = vec3(b_ada)
    norm_mix_w3, norm_ffn_w3 = vec3(norm_mix_w), vec3(norm_ffn_w)
    w_dt3 = pad_lanes(w_in[:, :, DT_COL0:])
    expand = (jnp.arange(LANES, dtype=I32)[:, None]
              == jnp.arange(SSD_WIDTH, dtype=I32)[None, :] // SSD_HEAD_DIM).astype(F32)
    ssd_consts = (conv_w, vec3(conv_b), vec3(pad_lanes(dt_bias)), vec3(pad_lanes(a_log)),
                  vec3(jnp.repeat(d_skip, SSD_HEAD_DIM, axis=-1)), vec3(ssd_norm_w), expand)
    weights = (norm_mix_w3, w_in, w_dt3, vec3(q_norm_w), vec3(k_norm_w), ssd_consts,
               w_out.reshape(depth, 2, MIX_WIDTH // 2, D_MODEL).reshape(depth * 2, MIX_WIDTH // 2, D_MODEL))
    router_wp = pad_lanes(router_w)
    router_bp = vec3(pad_lanes(router_b))
    w_gu = w_gate_up.reshape(depth * N_EXPERTS, D_MODEL, 2 * EXPERT_DIM)
    b_gu3 = b_gate_up.reshape(depth * N_EXPERTS, 1, 2 * EXPERT_DIM)
    w_dn = w_down.reshape(depth * N_EXPERTS, EXPERT_DIM, D_MODEL)
    b_dn3 = b_down.reshape(depth * N_EXPERTS, 1, D_MODEL)

    n_c = bp + bs
    c_all = jnp.concatenate([c_prompt, c_sample, jnp.zeros((-n_c % SUBLANES, D_MODEL), F32)], axis=0)
    pos_p = jnp.arange(seq, dtype=I32)
    pos_s = PAST_LEN + jnp.arange(dseq, dtype=I32)

    y_p = x_prompt
    y_s = x_sample.reshape(1, n_s, D_MODEL)
    outs = [[] for _ in range(8)]
    for layer in range(depth):
        mod = _ada(c_all, w_ada, b_ada3, layer)
        mod_p = mod[:bp].reshape(bp, 1, 6 * D_MODEL)
        mod_s = jnp.repeat(mod[bp:n_c], dseq, axis=0).reshape(1, n_s, 6 * D_MODEL)

        zero_conv = jnp.zeros((bp, CONV_WIDTH - 1, CONV_DIM), F32)
        zero_ssm = jnp.zeros((bp, SSD_HEADS, SSD_HEAD_DIM, SSD_STATE), F32)
        x1_p, kp, vp, cp, sp = _mixing(y_p, mod_p, bp, seq, pos_p, None, None, zero_conv, zero_ssm,
                                       weights, layer, 512)
        x1_s, kn, vn, cn, sn = _mixing(y_s, mod_s, bs, dseq, pos_s, cache_k, cache_v, state_conv[layer],
                                       state_ssm[layer], weights, layer, n_s)

        h2_p, idx_p, gate_p = _router(x1_p, norm_ffn_w3, layer, mod_p, 3, 4, router_wp, router_bp, 256)
        h2_s, idx_s, gate_s = _router(x1_s, norm_ffn_w3, layer, mod_s, 3, 4, router_wp, router_bp, n_s)
        top_idx = jnp.concatenate([idx_p.reshape(n_p, LANES)[:, :TOP_K], idx_s.reshape(n_s, LANES)[:, :TOP_K]], axis=0)
        slot, n_slots, tile_e, n_valid, next_e = _routing_tables(top_idx, n_tok)
        slot_p, slot_s = slot[:n_p * TOP_K], slot[n_p * TOP_K:]
        td = 256
        xs = jnp.zeros((n_slots, D_MODEL // 2), jnp.uint32)
        xs = _dispatch(slot_p.reshape(n_p // td, 1, td * TOP_K), h2_p.reshape(n_p, D_MODEL // 2), xs)
        xs = _dispatch(slot_s.reshape(1, 1, n_s * TOP_K), h2_s.reshape(n_s, D_MODEL // 2), xs)
        act = _moe_up(tile_e, n_valid, next_e, xs, w_gu, b_gu3, layer)
        y_slots = _moe_down(tile_e, n_valid, next_e, act, w_dn, b_dn3, layer)
        tt = 128
        y_p = _combine(slot_p.reshape(n_p // tt, 1, tt * TOP_K), y_slots, gate_p, x1_p, mod_p, 5, tt)
        y_s = _combine(slot_s.reshape(n_s // tt, 1, tt * TOP_K), y_slots, gate_s, x1_s, mod_s, 5, tt)

        for lst, val in zip(outs, (kp[:, seq - keep_p:], vp[:, seq - keep_p:], cp, sp, kn, vn, cn, sn)):
            lst.append(val)
    return (y_p, y_s.reshape(bs, dseq, D_MODEL)) + tuple(jnp.stack(o) for o in outs)
```
